```python
import jax, jax.numpy as jnp
from jax import lax
import numpy as np

D_MODEL = 1024
BATCH = 32
SEQ = 2048
DEPTH = 4

N_MIXERS = 2
N_LAYERS_A = (DEPTH + 1) // 2
N_LAYERS_B = DEPTH // 2
D_FF = 4 * D_MODEL
EPS = 1e-6
CHUNK = 128
A_WIDTH = 2 * D_MODEL
A_GROUPS = 8
A_GROUP_DIM = A_WIDTH // A_GROUPS
B_HEADS = 16
B_HEAD_DIM = D_MODEL // B_HEADS
Q_BLOCK = 128

kernel_name = "hybrid_gmlp_stickbreaking_trunk"


def rms_norm(x, g):
    xf = x.astype(jnp.float32)
    y = xf * lax.rsqrt(jnp.mean(xf * xf, axis=-1, keepdims=True) + EPS)
    return (y * g.astype(jnp.float32)).astype(x.dtype)


def layer_norm(x, g, b):
    xf = x.astype(jnp.float32)
    mu = jnp.mean(xf, axis=-1, keepdims=True)
    xc = xf - mu
    y = xc * lax.rsqrt(jnp.mean(xc * xc, axis=-1, keepdims=True) + EPS)
    return (y * g.astype(jnp.float32) + b.astype(jnp.float32)).astype(x.dtype)


def chunked_gmlp(x, w_in, v_g, v_b, w_s, b_s, w_out):
    bn, s, _ = x.shape
    z = jax.nn.gelu(x @ w_in, approximate=False)
    u, v = jnp.split(z, 2, axis=-1)
    v = layer_norm(v, v_g, v_b)
    nc = s // CHUNK
    v = v.reshape(bn, nc, CHUNK, A_GROUPS, A_GROUP_DIM)
    causal = jnp.tril(jnp.ones((CHUNK, CHUNK), dtype=bool))
    w_causal = jnp.where(causal[None], w_s, jnp.zeros_like(w_s)).astype(v.dtype)
    mixed = jnp.einsum('gts,bcsgd->bctgd', w_causal, v)
    mixed = mixed + b_s.T.astype(v.dtype)[None, None, :, :, None]
    gated = u * mixed.reshape(bn, s, A_WIDTH)
    return gated @ w_out


def stick_breaking_attention(x, w_qkv, w_out):
    bn, s, _ = x.shape
    qkv = (x @ w_qkv).reshape(bn, s, 3, B_HEADS, B_HEAD_DIM)
    q, k, v = qkv[:, :, 0], qkv[:, :, 1], qkv[:, :, 2]
    scale = B_HEAD_DIM ** -0.5
    outs = []
    for blk in range(s // Q_BLOCK):
        q0 = blk * Q_BLOCK
        kv_len = q0 + Q_BLOCK
        qb = q[:, q0:kv_len]
        kb = k[:, :kv_len]
        vb = v[:, :kv_len]
        z = jnp.einsum('bthd,bshd->bhts', qb, kb).astype(jnp.float32) * scale
        t_pos = q0 + jnp.arange(Q_BLOCK)[:, None]
        s_pos = jnp.arange(kv_len)[None, :]
        causal = s_pos < t_pos
        log_keep = jnp.where(causal, jax.nn.log_sigmoid(-z), 0.0)
        survive = lax.cumsum(log_keep, axis=log_keep.ndim - 1, reverse=True) - log_keep
        log_a = jax.nn.log_sigmoid(z) + survive
        a = jnp.where(causal, jnp.exp(log_a), 0.0)
        outs.append(jnp.einsum('bhts,bshd->bthd', a.astype(vb.dtype), vb))
    o = jnp.concatenate(outs, axis=1).reshape(bn, s, D_MODEL)
    return o @ w_out


def squared_relu_mlp(x, w1, w2):
    return jnp.square(jax.nn.relu(x @ w1)) @ w2


def setup_inputs(seed: int = 0) -> dict:
    key = jax.random.key(seed)
    ks = jax.random.split(key, 16)
    f32 = jnp.float32

    def nrm(k, shape, scale):
        return jax.random.normal(k, shape, f32) * scale

    x = jax.random.normal(ks[0], (BATCH, SEQ, D_MODEL), f32)
    norm_mix_pre = 1.0 + nrm(ks[1], (DEPTH, D_MODEL), 0.02)
    norm_mix_post = 1.0 + nrm(ks[2], (DEPTH, D_MODEL), 0.02)
    norm_ffn_pre = 1.0 + nrm(ks[3], (DEPTH, D_MODEL), 0.02)
    norm_ffn_post = 1.0 + nrm(ks[4], (DEPTH, D_MODEL), 0.02)
    a_w_in = nrm(ks[5], (N_LAYERS_A, D_MODEL, 2 * A_WIDTH), D_MODEL ** -0.5)
    a_v_g = 1.0 + nrm(ks[6], (N_LAYERS_A, A_WIDTH), 0.02)
    a_v_b = nrm(ks[7], (N_LAYERS_A, A_WIDTH), 0.02)
    a_w_s = nrm(ks[8], (N_LAYERS_A, A_GROUPS, CHUNK, CHUNK), CHUNK ** -0.5)
    a_b_s = 1.0 + nrm(ks[9], (N_LAYERS_A, A_GROUPS, CHUNK), 0.01)
    a_w_out = nrm(ks[10], (N_LAYERS_A, A_WIDTH, D_MODEL), A_WIDTH ** -0.5)
    b_w_qkv = nrm(ks[11], (N_LAYERS_B, D_MODEL, 3 * D_MODEL), D_MODEL ** -0.5)
    b_w_out = nrm(ks[12], (N_LAYERS_B, D_MODEL, D_MODEL), D_MODEL ** -0.5)
    mlp_w1 = nrm(ks[13], (DEPTH, D_MODEL, D_FF), D_MODEL ** -0.5)
    mlp_w2 = nrm(ks[14], (DEPTH, D_FF, D_MODEL), D_FF ** -0.5)
    return {"x": x, "norm_mix_pre": norm_mix_pre, "norm_mix_post": norm_mix_post,
            "norm_ffn_pre": norm_ffn_pre, "norm_ffn_post": norm_ffn_post,
            "a_w_in": a_w_in, "a_v_g": a_v_g, "a_v_b": a_v_b, "a_w_s": a_w_s,
            "a_b_s": a_b_s, "a_w_out": a_w_out, "b_w_qkv": b_w_qkv, "b_w_out": b_w_out,
            "mlp_w1": mlp_w1, "mlp_w2": mlp_w2}


def reference(x, norm_mix_pre, norm_mix_post, norm_ffn_pre, norm_ffn_post,
              a_w_in, a_v_g, a_v_b, a_w_s, a_b_s, a_w_out,
              b_w_qkv, b_w_out, mlp_w1, mlp_w2):
    for i in range(DEPTH):
        h = rms_norm(x, norm_mix_pre[i])
        j = i // N_MIXERS
        if i % N_MIXERS == 0:
            m = chunked_gmlp(h, a_w_in[j], a_v_g[j], a_v_b[j], a_w_s[j], a_b_s[j], a_w_out[j])
        else:
            m = stick_breaking_attention(h, b_w_qkv[j], b_w_out[j])
        x = x + rms_norm(m, norm_mix_post[i])
        h = rms_norm(x, norm_ffn_pre[i])
        f = squared_relu_mlp(h, mlp_w1[i], mlp_w2[i])
        x = x + rms_norm(f, norm_ffn_post[i])
    return x
```

```python
import functools
import math

import jax
import jax.numpy as jnp
from jax import lax
from jax.experimental import pallas as pl
from jax.experimental.pallas import tpu as pltpu

F32 = jnp.float32
BF16 = jnp.bfloat16

EPS = 1e-6
CHUNK = 128
A_GROUPS = 8
B_HEADS = 16
HEADS_PER_BLOCK = 2
ATT_TILE = 256
TOKEN_TILE = 256
VMEM_LIMIT_BYTES = 56 * 1024 * 1024


def _rms_norm(x, g):
    return x * lax.rsqrt(jnp.mean(x * x, axis=-1, keepdims=True) + EPS) * g


def _resident(shape):
    zeros = (0,) * len(shape)
    return pl.BlockSpec(shape, lambda *_: zeros, pipeline_mode=pl.Buffered(1))


def _params(*semantics):
    return pltpu.CompilerParams(dimension_semantics=semantics,
                                vmem_limit_bytes=VMEM_LIMIT_BYTES)


def _gmlp_kernel(x_ref, gpre_ref, gpost_ref, win_ref, vg_ref, vb_ref, ws_ref, bs_ref,
                 wout_ref, o_ref, gated_ref):
    tm = x_ref.shape[0]
    width = wout_ref.shape[0]
    gdim = width // A_GROUPS
    x = x_ref[...]
    h = _rms_norm(x, gpre_ref[...]).astype(BF16)
    z = jnp.dot(h, win_ref[...], preferred_element_type=F32)
    z = 0.5 * z * (1.0 + lax.erf(z * (1.0 / math.sqrt(2.0))))
    u = z[:, :width]
    v = z[:, width:]
    mu = jnp.mean(v, axis=-1, keepdims=True)
    vc = v - mu
    var = jnp.mean(vc * vc, axis=-1, keepdims=True)
    vn = (vc * lax.rsqrt(var + EPS) * vg_ref[...] + vb_ref[...]).astype(BF16)
    t_idx = lax.broadcasted_iota(jnp.int32, (CHUNK, CHUNK), 0)
    s_idx = lax.broadcasted_iota(jnp.int32, (CHUNK, CHUNK), 1)
    causal = t_idx >= s_idx
    for g in range(A_GROUPS):
        w = jnp.where(causal, ws_ref[g], 0.0).astype(BF16)
        cols = slice(g * gdim, (g + 1) * gdim)
        for c in range(tm // CHUNK):
            rows = slice(c * CHUNK, (c + 1) * CHUNK)
            mixed = jnp.dot(w, vn[rows, cols], preferred_element_type=F32) + bs_ref[:, cols]
            gated_ref[rows, cols] = (u[rows, cols] * mixed).astype(BF16)
    m = jnp.dot(gated_ref[...], wout_ref[...], preferred_element_type=F32)
    o_ref[...] = x + _rms_norm(m, gpost_ref[...])


def _gmlp_layer(x, gpre, gpost, w_in, v_g, v_b, w_s, bias_full, w_out):
    t, d = x.shape
    width = w_out.shape[0]
    tm = TOKEN_TILE
    row = pl.BlockSpec((tm, d), lambda i: (i, 0))
    return pl.pallas_call(
        _gmlp_kernel,
        out_shape=jax.ShapeDtypeStruct((t, d), F32),
        grid=(t // tm,),
        in_specs=[row, _resident((1, d)), _resident((1, d)), _resident(w_in.shape),
                  _resident((1, width)), _resident((1, width)), _resident(w_s.shape),
                  _resident(bias_full.shape), _resident(w_out.shape)],
        out_specs=row,
        scratch_shapes=[pltpu.VMEM((tm, width), BF16)],
        compiler_params=_params("parallel"),
        name="gmlp_mixer",
    )(x, gpre, gpost, w_in, v_g, v_b, w_s, bias_full, w_out)


def _ffn_kernel(*refs, with_attn_proj):
    if with_attn_proj:
        x_ref, a_ref, wo_ref, gmix_ref, gpre_ref, gpost_ref, w1_ref, w2_ref, o_ref = refs
        m = jnp.dot(a_ref[...], wo_ref[...], preferred_element_type=F32)
        x = x_ref[...] + _rms_norm(m, gmix_ref[...])
    else:
        x_ref, gpre_ref, gpost_ref, w1_ref, w2_ref, o_ref = refs
        x = x_ref[...]
    h = _rms_norm(x, gpre_ref[...]).astype(BF16)
    a = jnp.dot(h, w1_ref[...], preferred_element_type=F32)
    a = jnp.square(jnp.maximum(a, 0.0)).astype(BF16)
    f = jnp.dot(a, w2_ref[...], preferred_element_type=F32)
    o_ref[...] = x + _rms_norm(f, gpost_ref[...])


def _ffn_layer(x, gpre, gpost, w1, w2, attn=None):
    t, d = x.shape
    tm = TOKEN_TILE
    row = pl.BlockSpec((tm, d), lambda i: (i, 0))
    vec = _resident((1, d))
    if attn is None:
        args = (x, gpre, gpost, w1, w2)
        in_specs = [row, vec, vec, _resident(w1.shape), _resident(w2.shape)]
    else:
        o, w_o, gmix = attn
        args = (x, o, w_o, gmix, gpre, gpost, w1, w2)
        in_specs = [row, row, _resident(w_o.shape), vec, vec, vec,
                    _resident(w1.shape), _resident(w2.shape)]
    return pl.pallas_call(
        functools.partial(_ffn_kernel, with_attn_proj=attn is not None),
        out_shape=jax.ShapeDtypeStruct((t, d), F32),
        grid=(t // tm,),
        in_specs=in_specs,
        out_specs=row,
        compiler_params=_params("parallel"),
        name="ffn_attnproj" if attn is not None else "ffn",
    )(*args)


def _qkv_kernel(x_ref, g_ref, w_ref, q_ref, k_ref, v_ref, *, scale):
    d = x_ref.shape[1]
    h = _rms_norm(x_ref[...], g_ref[...]).astype(BF16)
    qkv = jnp.dot(h, w_ref[...], preferred_element_type=F32)
    q_ref[...] = (qkv[:, :d] * scale).astype(BF16)
    k_ref[...] = qkv[:, d:2 * d].astype(BF16)
    v_ref[...] = qkv[:, 2 * d:].astype(BF16)


def _qkv_proj(x, g, w_qkv, scale):
    t, d = x.shape
    tm = TOKEN_TILE
    row = pl.BlockSpec((tm, d), lambda i: (i, 0))
    out = jax.ShapeDtypeStruct((t, d), BF16)
    return pl.pallas_call(
        functools.partial(_qkv_kernel, scale=scale),
        out_shape=(out, out, out),
        grid=(t // tm,),
        in_specs=[row, _resident((1, d)), _resident(w_qkv.shape)],
        out_specs=(row, row, row),
        compiler_params=_params("parallel"),
        name="qkv_proj",
    )(x, g, w_qkv)


def _stick_tile(qh, k_t, v_t, upper, carry, strict_mask):
    z = lax.dot_general(qh, k_t, (((1,), (1,)), ((), ())), preferred_element_type=F32)
    log_keep = -(jnp.maximum(z, 0.0) + jnp.log(1.0 + jnp.exp(-jnp.abs(z))))
    if strict_mask is not None:
        log_keep = jnp.where(strict_mask, log_keep, 0.0)
    hi = log_keep.astype(BF16)
    lo = (log_keep - hi.astype(F32)).astype(BF16)
    sums = jnp.dot(jnp.concatenate([hi, lo], axis=0), upper, preferred_element_type=F32)
    suffix = sums[:ATT_TILE] + sums[ATT_TILE:]
    p = jnp.exp(z + suffix + jnp.concatenate([carry, carry], axis=1))
    if strict_mask is not None:
        p = jnp.where(strict_mask, p, 0.0)
    pv = jnp.dot(p.astype(BF16), v_t, preferred_element_type=F32)
    tile_sum = jnp.broadcast_to(suffix[:, 0:1], carry.shape)
    return pv, tile_sum


def _attn_kernel(q_ref, k_ref, v_ref, o_ref, acc_ref, carry_ref):
    seq, lanes = q_ref.shape
    head_dim = lanes // HEADS_PER_BLOCK
    tq = ATT_TILE
    lane = lax.broadcasted_iota(jnp.int32, (tq, lanes), 1)
    first_head = lane < head_dim
    j_idx = lax.broadcasted_iota(jnp.int32, (tq, tq), 0)
    s_idx = lax.broadcasted_iota(jnp.int32, (tq, tq), 1)
    upper = (j_idx >= s_idx).astype(BF16)
    strict = s_idx < j_idx

    def q_tile(qi, _):
        q0 = pl.multiple_of(qi * tq, tq)
        q = q_ref[pl.ds(q0, tq), :]
        zero = jnp.zeros_like(q)
        q_heads = (jnp.where(first_head, q, zero), jnp.where(first_head, zero, q))
        k_d = k_ref[pl.ds(q0, tq), :]
        v_d = v_ref[pl.ds(q0, tq), :]
        no_carry = jnp.zeros((tq, lanes), F32)
        for h in range(HEADS_PER_BLOCK):
            pv, tile_sum = _stick_tile(q_heads[h], k_d, v_d, upper, no_carry, strict)
            acc_ref[h] = pv
            carry_ref[h] = tile_sum

        def k_tile(i, _):
            k0 = pl.multiple_of((qi - 1 - i) * tq, tq)
            k_t = k_ref[pl.ds(k0, tq), :]
            v_t = v_ref[pl.ds(k0, tq), :]
            for h in range(HEADS_PER_BLOCK):
                pv, tile_sum = _stick_tile(q_heads[h], k_t, v_t, upper, carry_ref[h], None)
                acc_ref[h] += pv
                carry_ref[h] += tile_sum
            return 0

        lax.fori_loop(0, qi, k_tile, 0)
        o_ref[pl.ds(q0, tq), :] = jnp.where(first_head, acc_ref[0], acc_ref[1]).astype(o_ref.dtype)
        return 0

    lax.fori_loop(0, seq // tq, q_tile, 0)


def _attn_core(q, k, v, batch, seq):
    t, d = q.shape
    lanes = HEADS_PER_BLOCK * (d // B_HEADS)
    blk = pl.BlockSpec((seq, lanes), lambda b, hp: (b, hp))
    return pl.pallas_call(
        _attn_kernel,
        out_shape=jax.ShapeDtypeStruct((t, d), BF16),
        grid=(batch, d // lanes),
        in_specs=[blk, blk, blk],
        out_specs=blk,
        scratch_shapes=[pltpu.VMEM((HEADS_PER_BLOCK, ATT_TILE, lanes), F32),
                        pltpu.VMEM((HEADS_PER_BLOCK, ATT_TILE, lanes), F32)],
        compiler_params=_params("parallel", "parallel"),
        name="stick_attn",
    )(q, k, v)


def kernel(x, norm_mix_pre, norm_mix_post, norm_ffn_pre, norm_ffn_post, a_w_in, a_v_g, a_v_b, a_w_s, a_b_s, a_w_out, b_w_qkv, b_w_out, mlp_w1, mlp_w2):
    batch, seq, d = x.shape
    depth = norm_mix_pre.shape[0]
    width = a_w_out.shape[1]
    assert seq % ATT_TILE == 0 and (batch * seq) % TOKEN_TILE == 0 and TOKEN_TILE % CHUNK == 0
    assert a_w_s.shape[1:] == (A_GROUPS, CHUNK, CHUNK) and d // B_HEADS * HEADS_PER_BLOCK == 128
    scale = (d // B_HEADS) ** -0.5
    assert math.frexp(scale)[0] == 0.5

    row = lambda p, i: p[i][None, :]
    xt = x.reshape(batch * seq, d)
    for i in range(depth):
        j = i // 2
        gpre, gpost = row(norm_mix_pre, i), row(norm_mix_post, i)
        fpre, fpost = row(norm_ffn_pre, i), row(norm_ffn_post, i)
        w1, w2 = mlp_w1[i].astype(BF16), mlp_w2[i].astype(BF16)
        if i % 2 == 0:
            bias_full = jnp.repeat(a_b_s[j].T, width // A_GROUPS, axis=1)
            xt = _gmlp_layer(xt, gpre, gpost, a_w_in[j].astype(BF16), row(a_v_g, j), row(a_v_b, j),
                             a_w_s[j], bias_full, a_w_out[j].astype(BF16))
            xt = _ffn_layer(xt, fpre, fpost, w1, w2)
        else:
            q, k, v = _qkv_proj(xt, gpre, b_w_qkv[j].astype(BF16), scale)
            o = _attn_core(q, k, v, batch, seq)
            xt = _ffn_layer(xt, fpre, fpost, w1, w2, attn=(o, b_w_out[j].astype(BF16), gpost))
    return xt.reshape(batch, seq, d)
```

```python
import functools
import math

import jax
import jax.numpy as jnp
from jax import lax
from jax.experimental import pallas as pl
from jax.experimental.pallas import tpu as pltpu

F32 = jnp.float32
BF16 = jnp.bfloat16

EPS = 1e-6
LOG2_E = math.log2(math.e)
LANES = 128
CHUNK = 128
A_GROUPS = 8
B_HEADS = 16
HEADS_PER_BLOCK = 2
ATT_LANE_BLOCKS = 4
ATT_TILE = 256
TOKEN_TILE = 512
VMEM_LIMIT_BYTES = 56 * 1024 * 1024


def _rms_norm(x, g):
    return x * lax.rsqrt(jnp.mean(x * x, axis=-1, keepdims=True) + EPS) * g


def _resident(shape):
    zeros = (0,) * len(shape)
    return pl.BlockSpec(shape, lambda *_: zeros, pipeline_mode=pl.Buffered(1))


def _params(*semantics):
    return pltpu.CompilerParams(dimension_semantics=semantics,
                                vmem_limit_bytes=VMEM_LIMIT_BYTES)


def _gmlp_kernel(x_ref, gpre_ref, gpost_ref, win_ref, vg_ref, vb_ref, ws_ref, bs_ref,
                 wout_ref, o_ref, gated_ref):
    tm = x_ref.shape[0]
    width = wout_ref.shape[0]
    gdim = width // A_GROUPS
    x = x_ref[...]
    h = _rms_norm(x, gpre_ref[...]).astype(BF16)
    z = jnp.dot(h, win_ref[...], preferred_element_type=F32)
    z = 0.5 * z * (1.0 + lax.erf(z * (1.0 / math.sqrt(2.0))))
    u = z[:, :width]
    v = z[:, width:]
    mu = jnp.mean(v, axis=-1, keepdims=True)
    vc = v - mu
    var = jnp.mean(vc * vc, axis=-1, keepdims=True)
    vn = (vc * lax.rsqrt(var + EPS) * vg_ref[...] + vb_ref[...]).astype(BF16)
    t_idx = lax.broadcasted_iota(jnp.int32, (CHUNK, CHUNK), 0)
    s_idx = lax.broadcasted_iota(jnp.int32, (CHUNK, CHUNK), 1)
    causal = t_idx >= s_idx
    for g in range(A_GROUPS):
        w = jnp.where(causal, ws_ref[g], 0.0).astype(BF16)
        cols = slice(g * gdim, (g + 1) * gdim)
        for c in range(tm // CHUNK):
            rows = slice(c * CHUNK, (c + 1) * CHUNK)
            mixed = jnp.dot(w, vn[rows, cols], preferred_element_type=F32) + bs_ref[:, cols]
            gated_ref[rows, cols] = (u[rows, cols] * mixed).astype(BF16)
    m = jnp.dot(gated_ref[...], wout_ref[...], preferred_element_type=F32)
    o_ref[...] = x + _rms_norm(m, gpost_ref[...])


def _gmlp_layer(x, gpre, gpost, w_in, v_g, v_b, w_s, bias_full, w_out):
    t, d = x.shape
    width = w_out.shape[0]
    tm = TOKEN_TILE
    row = pl.BlockSpec((tm, d), lambda i: (i, 0))
    return pl.pallas_call(
        _gmlp_kernel,
        out_shape=jax.ShapeDtypeStruct((t, d), F32),
        grid=(t // tm,),
        in_specs=[row, _resident((1, d)), _resident((1, d)), _resident(w_in.shape),
                  _resident((1, width)), _resident((1, width)), _resident(w_s.shape),
                  _resident(bias_full.shape), _resident(w_out.shape)],
        out_specs=row,
        scratch_shapes=[pltpu.VMEM((tm, width), BF16)],
        compiler_params=_params("parallel"),
        name="gmlp_mixer",
    )(x, gpre, gpost, w_in, v_g, v_b, w_s, bias_full, w_out)


def _ffn_kernel(*refs, with_attn_proj):
    if with_attn_proj:
        x_ref, a_ref, wo_ref, gmix_ref, gpre_ref, gpost_ref, w1_ref, w2_ref, o_ref = refs
        m = jnp.dot(a_ref[...], wo_ref[...], preferred_element_type=F32)
        x = x_ref[...] + _rms_norm(m, gmix_ref[...])
    else:
        x_ref, gpre_ref, gpost_ref, w1_ref, w2_ref, o_ref = refs
        x = x_ref[...]
    h = _rms_norm(x, gpre_ref[...]).astype(BF16)
    a = jnp.dot(h, w1_ref[...], preferred_element_type=F32)
    a = jnp.square(jnp.maximum(a, 0.0)).astype(BF16)
    f = jnp.dot(a, w2_ref[...], preferred_element_type=F32)
    o_ref[...] = x + _rms_norm(f, gpost_ref[...])


def _ffn_layer(x, gpre, gpost, w1, w2, attn=None):
    t, d = x.shape
    tm = TOKEN_TILE
    row = pl.BlockSpec((tm, d), lambda i: (i, 0))
    vec = _resident((1, d))
    if attn is None:
        args = (x, gpre, gpost, w1, w2)
        in_specs = [row, vec, vec, _resident(w1.shape), _resident(w2.shape)]
    else:
        o, w_o, gmix = attn
        args = (x, o, w_o, gmix, gpre, gpost, w1, w2)
        in_specs = [row, row, _resident(w_o.shape), vec, vec, vec,
                    _resident(w1.shape), _resident(w2.shape)]
    return pl.pallas_call(
        functools.partial(_ffn_kernel, with_attn_proj=attn is not None),
        out_shape=jax.ShapeDtypeStruct((t, d), F32),
        grid=(t // tm,),
        in_specs=in_specs,
        out_specs=row,
        compiler_params=_params("parallel"),
        name="ffn_attnproj" if attn is not None else "ffn",
    )(*args)


def _qkv_kernel(x_ref, g_ref, w_ref, q_ref, k_ref, v_ref, *, scale):
    d = x_ref.shape[1]
    h = _rms_norm(x_ref[...], g_ref[...]).astype(BF16)
    qkv = jnp.dot(h, w_ref[...], preferred_element_type=F32)
    q_ref[...] = (qkv[:, :d] * scale).astype(BF16)
    k_ref[...] = qkv[:, d:2 * d].astype(BF16)
    v_ref[...] = qkv[:, 2 * d:].astype(BF16)


def _qkv_proj(x, g, w_qkv, scale):
    t, d = x.shape
    tm = TOKEN_TILE
    row = pl.BlockSpec((tm, d), lambda i: (i, 0))
    out = jax.ShapeDtypeStruct((t, d), BF16)
    return pl.pallas_call(
        functools.partial(_qkv_kernel, scale=scale),
        out_shape=(out, out, out),
        grid=(t // tm,),
        in_specs=[row, _resident((1, d)), _resident(w_qkv.shape)],
        out_specs=(row, row, row),
        compiler_params=_params("parallel"),
        name="qkv_proj",
    )(x, g, w_qkv)


def _stick_tiles(qs, ks, vs, upper2, carries, strict_mask):
    n = len(qs)
    contract_last = (((1,), (1,)), ((), ()))
    z2s = [lax.dot_general(qs[i], ks[i], contract_last, preferred_element_type=F32) * LOG2_E
           for i in range(n)]
    splits = []
    for z2 in z2s:
        neg_abs = lax.bitcast_convert_type(
            lax.bitcast_convert_type(z2, jnp.uint32) | jnp.uint32(0x80000000), F32)
        s2 = jnp.maximum(z2, 0.0) + jnp.log2(1.0 + jnp.exp2(neg_abs))
        if strict_mask is not None:
            s2 = jnp.where(strict_mask, s2, 0.0)
        hi = s2.astype(BF16)
        lo = (s2 - hi.astype(F32)).astype(BF16)
        splits.append(jnp.concatenate([hi, lo], axis=1))
    suffixes = [jnp.dot(s, upper2, preferred_element_type=F32) for s in splits]
    ps = []
    for z2, suffix, carry in zip(z2s, suffixes, carries):
        p = jnp.exp2(z2 - suffix - jnp.concatenate([carry, carry], axis=1))
        if strict_mask is not None:
            p = jnp.where(strict_mask, p, 0.0)
        ps.append(p.astype(BF16))
    pvs = [jnp.dot(ps[i], vs[i], preferred_element_type=F32) for i in range(n)]
    tile_sums = [jnp.broadcast_to(s[:, 0:1], carries[0].shape) for s in suffixes]
    return pvs, tile_sums


def _attn_kernel(q_ref, k_ref, v_ref, o_ref, qm_ref, acc_ref, carry_ref):
    seq = q_ref.shape[0]
    n_blocks = q_ref.shape[1] // LANES
    head_dim = LANES // HEADS_PER_BLOCK
    tq = ATT_TILE
    first_head = lax.broadcasted_iota(jnp.int32, (tq, LANES), 1) < head_dim
    j_idx = lax.broadcasted_iota(jnp.int32, (tq, tq), 0)
    s_idx = lax.broadcasted_iota(jnp.int32, (tq, tq), 1)
    upper = (j_idx >= s_idx).astype(BF16)
    upper2 = jnp.concatenate([upper, upper], axis=0)
    strict = s_idx < j_idx

    head_lane = lax.broadcasted_iota(jnp.int32, (seq, LANES), 1) < head_dim
    for blk in range(n_blocks):
        q = q_ref[:, blk * LANES:(blk + 1) * LANES]
        zero = jnp.zeros_like(q)
        qm_ref[HEADS_PER_BLOCK * blk] = jnp.where(head_lane, q, zero)
        qm_ref[HEADS_PER_BLOCK * blk + 1] = jnp.where(head_lane, zero, q)

    n_heads = n_blocks * HEADS_PER_BLOCK

    def tile(q0, k0, first):
        qs = [qm_ref[i, pl.ds(q0, tq), :] for i in range(n_heads)]
        ks = [k_ref[pl.ds(k0, tq), (i // HEADS_PER_BLOCK) * LANES:(i // HEADS_PER_BLOCK + 1) * LANES]
              for i in range(n_heads)]
        vs = [v_ref[pl.ds(k0, tq), (i // HEADS_PER_BLOCK) * LANES:(i // HEADS_PER_BLOCK + 1) * LANES]
              for i in range(n_heads)]
        if first:
            carries = [jnp.zeros((tq, LANES), F32)] * n_heads
        else:
            carries = [carry_ref[i] for i in range(n_heads)]
        pvs, tile_sums = _stick_tiles(qs, ks, vs, upper2, carries, strict if first else None)
        for i in range(n_heads):
            carry_ref[i] = tile_sums[i] if first else carries[i] + tile_sums[i]
        for blk in range(n_blocks):
            pv = jnp.where(first_head, pvs[HEADS_PER_BLOCK * blk], pvs[HEADS_PER_BLOCK * blk + 1])
            acc_ref[blk] = pv if first else acc_ref[blk] + pv

    def q_tile(qi, _):
        q0 = pl.multiple_of(qi * tq, tq)
        tile(q0, q0, True)

        def k_tile(i, _):
            tile(q0, pl.multiple_of((qi - 1 - i) * tq, tq), False)
            return 0

        lax.fori_loop(0, qi, k_tile, 0)
        for blk in range(n_blocks):
            o_ref[pl.ds(q0, tq), blk * LANES:(blk + 1) * LANES] = acc_ref[blk].astype(o_ref.dtype)
        return 0

    lax.fori_loop(0, seq // tq, q_tile, 0)


def _attn_core(q, k, v, batch, seq):
    t, d = q.shape
    assert LANES == HEADS_PER_BLOCK * (d // B_HEADS)
    width = ATT_LANE_BLOCKS * LANES
    blk = pl.BlockSpec((seq, width), lambda b, hp: (b, hp))
    n_heads = ATT_LANE_BLOCKS * HEADS_PER_BLOCK
    return pl.pallas_call(
        _attn_kernel,
        out_shape=jax.ShapeDtypeStruct((t, d), BF16),
        grid=(batch, d // width),
        in_specs=[blk, blk, blk],
        out_specs=blk,
        scratch_shapes=[pltpu.VMEM((n_heads, seq, LANES), BF16),
                        pltpu.VMEM((ATT_LANE_BLOCKS, ATT_TILE, LANES), F32),
                        pltpu.VMEM((n_heads, ATT_TILE, LANES), F32)],
        compiler_params=_params("parallel", "parallel"),
        name="stick_attn",
    )(q, k, v)


def kernel(x, norm_mix_pre, norm_mix_post, norm_ffn_pre, norm_ffn_post, a_w_in, a_v_g, a_v_b, a_w_s, a_b_s, a_w_out, b_w_qkv, b_w_out, mlp_w1, mlp_w2):
    batch, seq, d = x.shape
    depth = norm_mix_pre.shape[0]
    width = a_w_out.shape[1]
    assert seq % ATT_TILE == 0 and (batch * seq) % TOKEN_TILE == 0 and TOKEN_TILE % CHUNK == 0
    assert a_w_s.shape[1:] == (A_GROUPS, CHUNK, CHUNK)
    scale = (d // B_HEADS) ** -0.5
    assert math.frexp(scale)[0] == 0.5

    row = lambda p, i: p[i][None, :]
    xt = x.reshape(batch * seq, d)
    for i in range(depth):
        j = i // 2
        gpre, gpost = row(norm_mix_pre, i), row(norm_mix_post, i)
        fpre, fpost = row(norm_ffn_pre, i), row(norm_ffn_post, i)
        w1, w2 = mlp_w1[i].astype(BF16), mlp_w2[i].astype(BF16)
        if i % 2 == 0:
            bias_full = jnp.repeat(a_b_s[j].T, width // A_GROUPS, axis=1)
            xt = _gmlp_layer(xt, gpre, gpost, a_w_in[j].astype(BF16), row(a_v_g, j), row(a_v_b, j),
                             a_w_s[j], bias_full, a_w_out[j].astype(BF16))
            xt = _ffn_layer(xt, fpre, fpost, w1, w2)
        else:
            q, k, v = _qkv_proj(xt, gpre, b_w_qkv[j].astype(BF16), scale)
            o = _attn_core(q, k, v, batch, seq)
            xt = _ffn_layer(xt, fpre, fpost, w1, w2, attn=(o, b_w_out[j].astype(BF16), gpost))
    return xt.reshape(batch, seq, d)
```

```python
import functools
import math

import jax
import jax.numpy as jnp
from jax import lax
from jax.experimental import pallas as pl
from jax.experimental.pallas import tpu as pltpu

F32 = jnp.float32
BF16 = jnp.bfloat16

EPS = 1e-6
LOG2_E = math.log2(math.e)
LANES = 128
CHUNK = 128
A_GROUPS = 8
B_HEADS = 16
HEADS_PER_BLOCK = 2
ATT_LANE_BLOCKS = 4
ATT_TILE = 256
Q_HALVES = 2
TOKEN_TILE = 1024
SUB_TILE = 256
VMEM_LIMIT_BYTES = 56 * 1024 * 1024


def _rms_norm(x, g):
    return x * lax.rsqrt(jnp.mean(x * x, axis=-1, keepdims=True) + EPS) * g


def _resident(shape):
    zeros = (0,) * len(shape)
    return pl.BlockSpec(shape, lambda *_: zeros, pipeline_mode=pl.Buffered(1))


def _neg_abs(x):
    bits = lax.bitcast_convert_type(x, jnp.uint32) | jnp.uint32(0x80000000)
    return lax.bitcast_convert_type(bits, F32)


def _emit_pipelined(stages, n):
    for step in range(n + len(stages) - 1):
        for depth in reversed(range(len(stages))):
            i = step - depth
            if 0 <= i < n:
                stages[depth](i)


def _params(*semantics):
    return pltpu.CompilerParams(dimension_semantics=semantics,
                                vmem_limit_bytes=VMEM_LIMIT_BYTES)


def _gmlp_kernel(x_ref, gpre_ref, gpost_ref, win_ref, vg_ref, vb_ref, ws_ref, bs_ref,
                 wout_ref, o_ref, gated_ref):
    tm = x_ref.shape[0]
    width = wout_ref.shape[0]
    gdim = width // A_GROUPS
    t_idx = lax.broadcasted_iota(jnp.int32, (CHUNK, CHUNK), 0)
    s_idx = lax.broadcasted_iota(jnp.int32, (CHUNK, CHUNK), 1)
    causal = t_idx >= s_idx
    ws = [jnp.where(causal, ws_ref[g], 0.0).astype(BF16) for g in range(A_GROUPS)]
    xs, zs, us, vns = {}, {}, {}, {}

    def rows(i):
        return slice(i * SUB_TILE, (i + 1) * SUB_TILE)

    def in_proj(i):
        xs[i] = x_ref[rows(i), :]
        h = _rms_norm(xs[i], gpre_ref[...]).astype(BF16)
        zs[i] = jnp.dot(h, win_ref[...], preferred_element_type=F32)

    def gate_norm(i):
        z = zs[i]
        z = 0.5 * z * (1.0 + lax.erf(z * (1.0 / math.sqrt(2.0))))
        us[i] = z[:, :width]
        v = z[:, width:]
        mu = jnp.mean(v, axis=-1, keepdims=True)
        vc = v - mu
        var = jnp.mean(vc * vc, axis=-1, keepdims=True)
        vns[i] = (vc * lax.rsqrt(var + EPS) * vg_ref[...] + vb_ref[...]).astype(BF16)

    def mix(i):
        for g in range(A_GROUPS):
            cols = slice(g * gdim, (g + 1) * gdim)
            for c in range(SUB_TILE // CHUNK):
                r = slice(c * CHUNK, (c + 1) * CHUNK)
                mixed = jnp.dot(ws[g], vns[i][r, cols], preferred_element_type=F32) + bs_ref[:, cols]
                gated_ref[i * SUB_TILE + c * CHUNK:i * SUB_TILE + (c + 1) * CHUNK, cols] = (
                    us[i][r, cols] * mixed).astype(BF16)

    def out_proj(i):
        m = jnp.dot(gated_ref[rows(i), :], wout_ref[...], preferred_element_type=F32)
        o_ref[rows(i), :] = xs[i] + _rms_norm(m, gpost_ref[...])

    _emit_pipelined((in_proj, gate_norm, mix, out_proj), tm // SUB_TILE)


def _gmlp_layer(x, gpre, gpost, w_in, v_g, v_b, w_s, bias_full, w_out):
    t, d = x.shape
    width = w_out.shape[0]
    tm = TOKEN_TILE
    row = pl.BlockSpec((tm, d), lambda i: (i, 0))
    return pl.pallas_call(
        _gmlp_kernel,
        out_shape=jax.ShapeDtypeStruct((t, d), F32),
        grid=(t // tm,),
        in_specs=[row, _resident((1, d)), _resident((1, d)), _resident(w_in.shape),
                  _resident((1, width)), _resident((1, width)), _resident(w_s.shape),
                  _resident(bias_full.shape), _resident(w_out.shape)],
        out_specs=row,
        scratch_shapes=[pltpu.VMEM((tm, width), BF16)],
        compiler_params=_params("parallel"),
        name="gmlp_mixer",
    )(x, gpre, gpost, w_in, v_g, v_b, w_s, bias_full, w_out)


def _ffn_kernel(*refs, with_attn_proj):
    if with_attn_proj:
        x_ref, a_ref, wo_ref, gmix_ref, gpre_ref, gpost_ref, w1_ref, w2_ref, o_ref = refs
    else:
        x_ref, gpre_ref, gpost_ref, w1_ref, w2_ref, o_ref = refs
    ms, xs, hidden = {}, {}, {}

    def rows(i):
        return slice(i * SUB_TILE, (i + 1) * SUB_TILE)

    def attn_proj(i):
        ms[i] = jnp.dot(a_ref[rows(i), :], wo_ref[...], preferred_element_type=F32)

    def up_proj(i):
        xs[i] = x_ref[rows(i), :]
        if with_attn_proj:
            xs[i] = xs[i] + _rms_norm(ms[i], gmix_ref[...])
        h = _rms_norm(xs[i], gpre_ref[...]).astype(BF16)
        hidden[i] = jnp.dot(h, w1_ref[...], preferred_element_type=F32)

    def down_proj(i):
        a = jnp.square(jnp.maximum(hidden[i], 0.0)).astype(BF16)
        f = jnp.dot(a, w2_ref[...], preferred_element_type=F32)
        o_ref[rows(i), :] = xs[i] + _rms_norm(f, gpost_ref[...])

    stages = (attn_proj, up_proj, down_proj) if with_attn_proj else (up_proj, down_proj)
    _emit_pipelined(stages, x_ref.shape[0] // SUB_TILE)


def _ffn_layer(x, gpre, gpost, w1, w2, attn=None):
    t, d = x.shape
    tm = TOKEN_TILE
    row = pl.BlockSpec((tm, d), lambda i: (i, 0))
    vec = _resident((1, d))
    if attn is None:
        args = (x, gpre, gpost, w1, w2)
        in_specs = [row, vec, vec, _resident(w1.shape), _resident(w2.shape)]
    else:
        o, w_o, gmix = attn
        args = (x, o, w_o, gmix, gpre, gpost, w1, w2)
        in_specs = [row, row, _resident(w_o.shape), vec, vec, vec,
                    _resident(w1.shape), _resident(w2.shape)]
    return pl.pallas_call(
        functools.partial(_ffn_kernel, with_attn_proj=attn is not None),
        out_shape=jax.ShapeDtypeStruct((t, d), F32),
        grid=(t // tm,),
        in_specs=in_specs,
        out_specs=row,
        compiler_params=_params("parallel"),
        name="ffn_attnproj" if attn is not None else "ffn",
    )(*args)


def _qkv_kernel(x_ref, g_ref, w_ref, q_ref, k_ref, v_ref, *, q_scale):
    d = x_ref.shape[1]
    h = _rms_norm(x_ref[...], g_ref[...]).astype(BF16)
    qkv = jnp.dot(h, w_ref[...], preferred_element_type=F32)
    q_ref[...] = (qkv[:, :d] * q_scale).astype(BF16)
    k_ref[...] = qkv[:, d:2 * d].astype(BF16)
    v_ref[...] = qkv[:, 2 * d:].astype(BF16)


def _qkv_proj(x, g, w_qkv, q_scale):
    t, d = x.shape
    tm = TOKEN_TILE
    row = pl.BlockSpec((tm, d), lambda i: (i, 0))
    out = jax.ShapeDtypeStruct((t, d), BF16)
    return pl.pallas_call(
        functools.partial(_qkv_kernel, q_scale=q_scale),
        out_shape=(out, out, out),
        grid=(t // tm,),
        in_specs=[row, _resident((1, d)), _resident(w_qkv.shape)],
        out_specs=(row, row, row),
        compiler_params=_params("parallel"),
        name="qkv_proj",
    )(x, g, w_qkv)


def _stick_tiles(qs, ks, vs, upper, masks):
    n = len(qs)
    contract_last = (((1,), (1,)), ((), ()))
    z2s, s2_bf, log_sig, first_col, later, ps, pvs = ({} for _ in range(7))

    def scores(i):
        z2s[i] = lax.dot_general(qs[i], ks[i], contract_last, preferred_element_type=F32)

    def softplus(i):
        z2 = z2s[i]
        s2 = jnp.maximum(z2, 0.0) + jnp.log2(1.0 + jnp.exp2(_neg_abs(z2)))
        log_sig[i] = z2 - s2
        if masks[i] is not None:
            s2 = jnp.where(masks[i], s2, 0.0)
        s2_bf[i] = s2.astype(BF16)
        first_col[i] = s2[:, 0:1]

    def later_sums(i):
        later[i] = jnp.dot(s2_bf[i], upper, preferred_element_type=F32)

    def weights(i):
        p = jnp.exp2(log_sig[i] - later[i])
        if masks[i] is not None:
            p = jnp.where(masks[i], p, 0.0)
        ps[i] = p.astype(BF16)

    def values(i):
        pvs[i] = jnp.dot(ps[i], vs[i], preferred_element_type=F32)

    _emit_pipelined((scores, softplus, later_sums, weights, values), n)
    tile_sums = [first_col[i] + later[i][:, 0:1] for i in range(n)]
    return [pvs[i] for i in range(n)], tile_sums


def _attn_kernel(q_ref, k_ref, v_ref, o_ref, qm_ref, acc_ref, carry_ref):
    seq = q_ref.shape[0]
    n_blocks = q_ref.shape[1] // LANES
    head_dim = LANES // HEADS_PER_BLOCK
    tq = ATT_TILE
    first_head = lax.broadcasted_iota(jnp.int32, (tq, LANES), 1) < head_dim
    j_idx = lax.broadcasted_iota(jnp.int32, (tq, tq), 0)
    s_idx = lax.broadcasted_iota(jnp.int32, (tq, tq), 1)
    upper = (j_idx > s_idx).astype(BF16)
    strict = s_idx < j_idx

    head_lane = lax.broadcasted_iota(jnp.int32, (seq, LANES), 1) < head_dim
    for blk in range(n_blocks):
        q = q_ref[:, blk * LANES:(blk + 1) * LANES]
        zero = jnp.zeros_like(q)
        qm_ref[HEADS_PER_BLOCK * blk] = jnp.where(head_lane, q, zero)
        qm_ref[HEADS_PER_BLOCK * blk + 1] = jnp.where(head_lane, zero, q)

    def run(q0, items):
        qs, ks, vs, masks = [], [], [], []
        for half, blk, k0, diagonal, _ in items:
            lanes = slice(blk * LANES, (blk + 1) * LANES)
            for h in range(HEADS_PER_BLOCK):
                qs.append(qm_ref[HEADS_PER_BLOCK * blk + h, pl.ds(q0 + half * tq, tq), :])
                ks.append(k_ref[pl.ds(k0, tq), lanes])
                vs.append(v_ref[pl.ds(k0, tq), lanes])
                masks.append(strict if diagonal else None)
        pvs, tile_sums = _stick_tiles(qs, ks, vs, upper, masks)
        acc, carry = {}, {}
        for n, (half, blk, _, _, first) in enumerate(items):
            heads = [HEADS_PER_BLOCK * blk + h for h in range(HEADS_PER_BLOCK)]
            pv = jnp.where(first_head, pvs[HEADS_PER_BLOCK * n], pvs[HEADS_PER_BLOCK * n + 1])
            if first:
                acc[half, blk] = pv
            else:
                for h in heads:
                    if (half, h) not in carry:
                        carry[half, h] = carry_ref[half, h]
                if (half, blk) not in acc:
                    acc[half, blk] = acc_ref[half, blk]
                decay = jnp.exp2(jnp.where(first_head, carry[half, heads[0]], carry[half, heads[1]]))
                acc[half, blk] = acc[half, blk] + decay * pv
            for j, h in enumerate(heads):
                tile_sum = jnp.broadcast_to(tile_sums[HEADS_PER_BLOCK * n + j], (tq, LANES))
                carry[half, h] = -tile_sum if first else carry[half, h] - tile_sum
        for key, value in acc.items():
            acc_ref[key] = value
        for key, value in carry.items():
            carry_ref[key] = value

    blocks = range(n_blocks)

    def q_tile(m, _):
        q0 = pl.multiple_of(m * (Q_HALVES * tq), Q_HALVES * tq)
        k_hi = pl.multiple_of(q0 + tq, tq)
        run(q0, [(1, b, k_hi, True, True) for b in blocks]
                + [(0, b, q0, True, True) for b in blocks]
                + [(1, b, q0, False, False) for b in blocks])

        def k_tile(i, _):
            k0 = pl.multiple_of((Q_HALVES * m - 1 - i) * tq, tq)
            run(q0, [(half, b, k0, False, False) for b in blocks for half in range(Q_HALVES)])
            return 0

        lax.fori_loop(0, Q_HALVES * m, k_tile, 0)
        for half in range(Q_HALVES):
            for b in blocks:
                o_ref[pl.ds(q0 + half * tq, tq), b * LANES:(b + 1) * LANES] = (
                    acc_ref[half, b].astype(o_ref.dtype))
        return 0

    lax.fori_loop(0, seq // (Q_HALVES * tq), q_tile, 0)


def _attn_core(q, k, v, batch, seq):
    t, d = q.shape
    assert LANES == HEADS_PER_BLOCK * (d // B_HEADS)
    width = ATT_LANE_BLOCKS * LANES
    blk = pl.BlockSpec((seq, width), lambda b, hp: (b, hp))
    n_heads = ATT_LANE_BLOCKS * HEADS_PER_BLOCK
    return pl.pallas_call(
        _attn_kernel,
        out_shape=jax.ShapeDtypeStruct((t, d), BF16),
        grid=(batch, d // width),
        in_specs=[blk, blk, blk],
        out_specs=blk,
        scratch_shapes=[pltpu.VMEM((n_heads, seq, LANES), BF16),
                        pltpu.VMEM((Q_HALVES, ATT_LANE_BLOCKS, ATT_TILE, LANES), F32),
                        pltpu.VMEM((Q_HALVES, n_heads, ATT_TILE, LANES), F32)],
        compiler_params=_params("parallel", "parallel"),
        name="stick_attn",
    )(q, k, v)


def kernel(x, norm_mix_pre, norm_mix_post, norm_ffn_pre, norm_ffn_post, a_w_in, a_v_g, a_v_b, a_w_s, a_b_s, a_w_out, b_w_qkv, b_w_out, mlp_w1, mlp_w2):
    batch, seq, d = x.shape
    depth = norm_mix_pre.shape[0]
    width = a_w_out.shape[1]
    assert seq % (Q_HALVES * ATT_TILE) == 0 and (batch * seq) % TOKEN_TILE == 0 and TOKEN_TILE % CHUNK == 0
    assert a_w_s.shape[1:] == (A_GROUPS, CHUNK, CHUNK)
    q_scale = (d // B_HEADS) ** -0.5 * LOG2_E

    row = lambda p, i: p[i][None, :]
    xt = x.reshape(batch * seq, d)
    for i in range(depth):
        j = i // 2
        gpre, gpost = row(norm_mix_pre, i), row(norm_mix_post, i)
        fpre, fpost = row(norm_ffn_pre, i), row(norm_ffn_post, i)
        w1, w2 = mlp_w1[i].astype(BF16), mlp_w2[i].astype(BF16)
        if i % 2 == 0:
            bias_full = jnp.repeat(a_b_s[j].T, width // A_GROUPS, axis=1)
            xt = _gmlp_layer(xt, gpre, gpost, a_w_in[j].astype(BF16), row(a_v_g, j), row(a_v_b, j),
                             a_w_s[j], bias_full, a_w_out[j].astype(BF16))
            xt = _ffn_layer(xt, fpre, fpost, w1, w2)
        else:
            q, k, v = _qkv_proj(xt, gpre, b_w_qkv[j].astype(BF16), q_scale)
            o = _attn_core(q, k, v, batch, seq)
            xt = _ffn_layer(xt, fpre, fpost, w1, w2, attn=(o, b_w_out[j].astype(BF16), gpost))
    return xt.reshape(batch, seq, d)
```

```python
import functools
import math

import jax
import jax.numpy as jnp
from jax import lax
from jax.experimental import pallas as pl
from jax.experimental.pallas import tpu as pltpu

F32 = jnp.float32
BF16 = jnp.bfloat16

EPS = 1e-6
LOG2_E = math.log2(math.e)
LANES = 128
CHUNK = 128
A_GROUPS = 8
B_HEADS = 16
HEADS_PER_BLOCK = 2
ATT_LANE_BLOCKS = 4
ATT_TILE = 256
Q_HALVES = 2
DEAD_CARRY = -160.0
TOKEN_TILE = 1024
SUB_TILE = 256
VMEM_LIMIT_BYTES = 56 * 1024 * 1024


def _rms_norm(x, g):
    return x * lax.rsqrt(jnp.mean(x * x, axis=-1, keepdims=True) + EPS) * g


def _resident(shape):
    zeros = (0,) * len(shape)
    return pl.BlockSpec(shape, lambda *_: zeros, pipeline_mode=pl.Buffered(1))


def _neg_abs(x):
    bits = lax.bitcast_convert_type(x, jnp.uint32) | jnp.uint32(0x80000000)
    return lax.bitcast_convert_type(bits, F32)


def _emit_pipelined(stages, n):
    for step in range(n + len(stages) - 1):
        for depth in reversed(range(len(stages))):
            i = step - depth
            if 0 <= i < n:
                stages[depth](i)


def _params(*semantics):
    return pltpu.CompilerParams(dimension_semantics=semantics,
                                vmem_limit_bytes=VMEM_LIMIT_BYTES)


def _gmlp_kernel(x_ref, gpre_ref, gpost_ref, win_ref, vg_ref, vb_ref, ws_ref, bs_ref,
                 wout_ref, o_ref, gated_ref):
    tm = x_ref.shape[0]
    width = wout_ref.shape[0]
    gdim = width // A_GROUPS
    t_idx = lax.broadcasted_iota(jnp.int32, (CHUNK, CHUNK), 0)
    s_idx = lax.broadcasted_iota(jnp.int32, (CHUNK, CHUNK), 1)
    causal = t_idx >= s_idx
    ws = [jnp.where(causal, ws_ref[g], 0.0).astype(BF16) for g in range(A_GROUPS)]
    xs, zs, us, vns = {}, {}, {}, {}

    def rows(i):
        return slice(i * SUB_TILE, (i + 1) * SUB_TILE)

    def in_proj(i):
        xs[i] = x_ref[rows(i), :]
        h = _rms_norm(xs[i], gpre_ref[...]).astype(BF16)
        zs[i] = jnp.dot(h, win_ref[...], preferred_element_type=F32)

    def gate_norm(i):
        z = zs[i]
        z = 0.5 * z * (1.0 + lax.erf(z * (1.0 / math.sqrt(2.0))))
        us[i] = z[:, :width]
        v = z[:, width:]
        mu = jnp.mean(v, axis=-1, keepdims=True)
        vc = v - mu
        var = jnp.mean(vc * vc, axis=-1, keepdims=True)
        vns[i] = (vc * lax.rsqrt(var + EPS) * vg_ref[...] + vb_ref[...]).astype(BF16)

    def mix(i):
        for g in range(A_GROUPS):
            cols = slice(g * gdim, (g + 1) * gdim)
            for c in range(SUB_TILE // CHUNK):
                r = slice(c * CHUNK, (c + 1) * CHUNK)
                mixed = jnp.dot(ws[g], vns[i][r, cols], preferred_element_type=F32) + bs_ref[:, cols]
                gated_ref[i * SUB_TILE + c * CHUNK:i * SUB_TILE + (c + 1) * CHUNK, cols] = (
                    us[i][r, cols] * mixed).astype(BF16)

    def out_proj(i):
        m = jnp.dot(gated_ref[rows(i), :], wout_ref[...], preferred_element_type=F32)
        o_ref[rows(i), :] = xs[i] + _rms_norm(m, gpost_ref[...])

    _emit_pipelined((in_proj, gate_norm, mix, out_proj), tm // SUB_TILE)


def _gmlp_layer(x, gpre, gpost, w_in, v_g, v_b, w_s, bias_full, w_out):
    t, d = x.shape
    width = w_out.shape[0]
    tm = TOKEN_TILE
    row = pl.BlockSpec((tm, d), lambda i: (i, 0))
    return pl.pallas_call(
        _gmlp_kernel,
        out_shape=jax.ShapeDtypeStruct((t, d), F32),
        grid=(t // tm,),
        in_specs=[row, _resident((1, d)), _resident((1, d)), _resident(w_in.shape),
                  _resident((1, width)), _resident((1, width)), _resident(w_s.shape),
                  _resident(bias_full.shape), _resident(w_out.shape)],
        out_specs=row,
        scratch_shapes=[pltpu.VMEM((tm, width), BF16)],
        compiler_params=_params("parallel"),
        name="gmlp_mixer",
    )(x, gpre, gpost, w_in, v_g, v_b, w_s, bias_full, w_out)


def _ffn_kernel(*refs, with_attn_proj):
    if with_attn_proj:
        x_ref, a_ref, wo_ref, gmix_ref, gpre_ref, gpost_ref, w1_ref, w2_ref, o_ref = refs
    else:
        x_ref, gpre_ref, gpost_ref, w1_ref, w2_ref, o_ref = refs
    ms, xs, hidden = {}, {}, {}

    def rows(i):
        return slice(i * SUB_TILE, (i + 1) * SUB_TILE)

    def attn_proj(i):
        ms[i] = jnp.dot(a_ref[rows(i), :], wo_ref[...], preferred_element_type=F32)

    def up_proj(i):
        xs[i] = x_ref[rows(i), :]
        if with_attn_proj:
            xs[i] = xs[i] + _rms_norm(ms[i], gmix_ref[...])
        h = _rms_norm(xs[i], gpre_ref[...]).astype(BF16)
        hidden[i] = jnp.dot(h, w1_ref[...], preferred_element_type=F32)

    def down_proj(i):
        a = jnp.square(jnp.maximum(hidden[i], 0.0)).astype(BF16)
        f = jnp.dot(a, w2_ref[...], preferred_element_type=F32)
        o_ref[rows(i), :] = xs[i] + _rms_norm(f, gpost_ref[...])

    stages = (attn_proj, up_proj, down_proj) if with_attn_proj else (up_proj, down_proj)
    _emit_pipelined(stages, x_ref.shape[0] // SUB_TILE)


def _ffn_layer(x, gpre, gpost, w1, w2, attn=None):
    t, d = x.shape
    tm = TOKEN_TILE
    row = pl.BlockSpec((tm, d), lambda i: (i, 0))
    vec = _resident((1, d))
    if attn is None:
        args = (x, gpre, gpost, w1, w2)
        in_specs = [row, vec, vec, _resident(w1.shape), _resident(w2.shape)]
    else:
        o, w_o, gmix = attn
        args = (x, o, w_o, gmix, gpre, gpost, w1, w2)
        in_specs = [row, row, _resident(w_o.shape), vec, vec, vec,
                    _resident(w1.shape), _resident(w2.shape)]
    return pl.pallas_call(
        functools.partial(_ffn_kernel, with_attn_proj=attn is not None),
        out_shape=jax.ShapeDtypeStruct((t, d), F32),
        grid=(t // tm,),
        in_specs=in_specs,
        out_specs=row,
        compiler_params=_params("parallel"),
        name="ffn_attnproj" if attn is not None else "ffn",
    )(*args)


def _qkv_kernel(x_ref, g_ref, w_ref, q_ref, k_ref, v_ref, *, q_scale):
    d = x_ref.shape[1]
    h = _rms_norm(x_ref[...], g_ref[...]).astype(BF16)
    qkv = jnp.dot(h, w_ref[...], preferred_element_type=F32)
    q_ref[...] = (qkv[:, :d] * q_scale).astype(BF16)
    k_ref[...] = qkv[:, d:2 * d].astype(BF16)
    v_ref[...] = qkv[:, 2 * d:].astype(BF16)


def _qkv_proj(x, g, w_qkv, q_scale):
    t, d = x.shape
    tm = TOKEN_TILE
    row = pl.BlockSpec((tm, d), lambda i: (i, 0))
    out = jax.ShapeDtypeStruct((t, d), BF16)
    return pl.pallas_call(
        functools.partial(_qkv_kernel, q_scale=q_scale),
        out_shape=(out, out, out),
        grid=(t // tm,),
        in_specs=[row, _resident((1, d)), _resident(w_qkv.shape)],
        out_specs=(row, row, row),
        compiler_params=_params("parallel"),
        name="qkv_proj",
    )(x, g, w_qkv)


def _stick_tiles(qs, ks, vs, upper, masks):
    n = len(qs)
    contract_last = (((1,), (1,)), ((), ()))
    z2s, s2_bf, log_sig, first_col, later, ps, pvs = ({} for _ in range(7))

    def scores(i):
        z2s[i] = lax.dot_general(qs[i], ks[i], contract_last, preferred_element_type=F32)

    def softplus(i):
        z2 = z2s[i]
        s2 = jnp.maximum(z2, 0.0) + jnp.log2(1.0 + jnp.exp2(_neg_abs(z2)))
        log_sig[i] = z2 - s2
        if masks[i] is not None:
            s2 = jnp.where(masks[i], s2, 0.0)
        s2_bf[i] = s2.astype(BF16)
        first_col[i] = s2[:, 0:1]

    def later_sums(i):
        later[i] = jnp.dot(s2_bf[i], upper, preferred_element_type=F32)

    def weights(i):
        p = jnp.exp2(log_sig[i] - later[i])
        if masks[i] is not None:
            p = jnp.where(masks[i], p, 0.0)
        ps[i] = p.astype(BF16)

    def values(i):
        pvs[i] = jnp.dot(ps[i], vs[i], preferred_element_type=F32)

    _emit_pipelined((scores, softplus, later_sums, weights, values), n)
    tile_sums = [first_col[i] + later[i][:, 0:1] for i in range(n)]
    return [pvs[i] for i in range(n)], tile_sums


def _attn_kernel(q_ref, k_ref, v_ref, o_ref, qm_ref, acc_ref, carry_ref):
    seq = q_ref.shape[0]
    n_blocks = q_ref.shape[1] // LANES
    head_dim = LANES // HEADS_PER_BLOCK
    tq = ATT_TILE
    first_head = lax.broadcasted_iota(jnp.int32, (tq, LANES), 1) < head_dim
    j_idx = lax.broadcasted_iota(jnp.int32, (tq, tq), 0)
    s_idx = lax.broadcasted_iota(jnp.int32, (tq, tq), 1)
    upper = (j_idx > s_idx).astype(BF16)
    strict = s_idx < j_idx

    head_lane = lax.broadcasted_iota(jnp.int32, (seq, LANES), 1) < head_dim
    for blk in range(n_blocks):
        q = q_ref[:, blk * LANES:(blk + 1) * LANES]
        zero = jnp.zeros_like(q)
        qm_ref[HEADS_PER_BLOCK * blk] = jnp.where(head_lane, q, zero)
        qm_ref[HEADS_PER_BLOCK * blk + 1] = jnp.where(head_lane, zero, q)

    def run(q0, items):
        qs, ks, vs, masks = [], [], [], []
        for half, blk, k0, diagonal, _ in items:
            lanes = slice(blk * LANES, (blk + 1) * LANES)
            for h in range(HEADS_PER_BLOCK):
                qs.append(qm_ref[HEADS_PER_BLOCK * blk + h, pl.ds(q0 + half * tq, tq), :])
                ks.append(k_ref[pl.ds(k0, tq), lanes])
                vs.append(v_ref[pl.ds(k0, tq), lanes])
                masks.append(strict if diagonal else None)
        pvs, tile_sums = _stick_tiles(qs, ks, vs, upper, masks)
        acc, carry = {}, {}
        for n, (half, blk, _, _, first) in enumerate(items):
            heads = [HEADS_PER_BLOCK * blk + h for h in range(HEADS_PER_BLOCK)]
            pv = jnp.where(first_head, pvs[HEADS_PER_BLOCK * n], pvs[HEADS_PER_BLOCK * n + 1])
            if first:
                acc[half, blk] = pv
            else:
                for h in heads:
                    if (half, h) not in carry:
                        carry[half, h] = carry_ref[half, h]
                if (half, blk) not in acc:
                    acc[half, blk] = acc_ref[half, blk]
                decay = jnp.exp2(jnp.where(first_head, carry[half, heads[0]], carry[half, heads[1]]))
                acc[half, blk] = acc[half, blk] + decay * pv
            for j, h in enumerate(heads):
                tile_sum = jnp.broadcast_to(tile_sums[HEADS_PER_BLOCK * n + j], (tq, LANES))
                carry[half, h] = -tile_sum if first else carry[half, h] - tile_sum
        for key, value in acc.items():
            acc_ref[key] = value
        for key, value in carry.items():
            carry_ref[key] = value
        return functools.reduce(jnp.maximum, carry.values())

    blocks = range(n_blocks)

    def q_tile(m, _):
        q0 = pl.multiple_of(m * (Q_HALVES * tq), Q_HALVES * tq)
        k_hi = pl.multiple_of(q0 + tq, tq)
        run(q0, [(1, b, k_hi, True, True) for b in blocks]
                + [(0, b, q0, True, True) for b in blocks]
                + [(1, b, q0, False, False) for b in blocks])

        def k_tile(state):
            i, _ = state
            k0 = pl.multiple_of((Q_HALVES * m - 1 - i) * tq, tq)
            top = run(q0, [(half, b, k0, False, False) for b in blocks for half in range(Q_HALVES)])
            return i + 1, jnp.max(top) > DEAD_CARRY

        lax.while_loop(lambda state: jnp.logical_and(state[0] < Q_HALVES * m, state[1]),
                       k_tile, (jnp.int32(0), jnp.bool_(True)))
        for half in range(Q_HALVES):
            for b in blocks:
                o_ref[pl.ds(q0 + half * tq, tq), b * LANES:(b + 1) * LANES] = (
                    acc_ref[half, b].astype(o_ref.dtype))
        return 0

    lax.fori_loop(0, seq // (Q_HALVES * tq), q_tile, 0)


def _attn_core(q, k, v, batch, seq):
    t, d = q.shape
    assert LANES == HEADS_PER_BLOCK * (d // B_HEADS)
    width = ATT_LANE_BLOCKS * LANES
    blk = pl.BlockSpec((seq, width), lambda b, hp: (b, hp))
    n_heads = ATT_LANE_BLOCKS * HEADS_PER_BLOCK
    return pl.pallas_call(
        _attn_kernel,
        out_shape=jax.ShapeDtypeStruct((t, d), BF16),
        grid=(batch, d // width),
        in_specs=[blk, blk, blk],
        out_specs=blk,
        scratch_shapes=[pltpu.VMEM((n_heads, seq, LANES), BF16),
                        pltpu.VMEM((Q_HALVES, ATT_LANE_BLOCKS, ATT_TILE, LANES), F32),
                        pltpu.VMEM((Q_HALVES, n_heads, ATT_TILE, LANES), F32)],
        compiler_params=_params("parallel", "parallel"),
        name="stick_attn",
    )(q, k, v)


def kernel(x, norm_mix_pre, norm_mix_post, norm_ffn_pre, norm_ffn_post, a_w_in, a_v_g, a_v_b, a_w_s, a_b_s, a_w_out, b_w_qkv, b_w_out, mlp_w1, mlp_w2):
    batch, seq, d = x.shape
    depth = norm_mix_pre.shape[0]
    width = a_w_out.shape[1]
    assert seq % (Q_HALVES * ATT_TILE) == 0 and (batch * seq) % TOKEN_TILE == 0 and TOKEN_TILE % CHUNK == 0
    assert a_w_s.shape[1:] == (A_GROUPS, CHUNK, CHUNK)
    q_scale = (d // B_HEADS) ** -0.5 * LOG2_E

    row = lambda p, i: p[i][None, :]
    xt = x.reshape(batch * seq, d)
    for i in range(depth):
        j = i // 2
        gpre, gpost = row(norm_mix_pre, i), row(norm_mix_post, i)
        fpre, fpost = row(norm_ffn_pre, i), row(norm_ffn_post, i)
        w1, w2 = mlp_w1[i].astype(BF16), mlp_w2[i].astype(BF16)
        if i % 2 == 0:
            bias_full = jnp.repeat(a_b_s[j].T, width // A_GROUPS, axis=1)
            xt = _gmlp_layer(xt, gpre, gpost, a_w_in[j].astype(BF16), row(a_v_g, j), row(a_v_b, j),
                             a_w_s[j], bias_full, a_w_out[j].astype(BF16))
            xt = _ffn_layer(xt, fpre, fpost, w1, w2)
        else:
            q, k, v = _qkv_proj(xt, gpre, b_w_qkv[j].astype(BF16), q_scale)
            o = _attn_core(q, k, v, batch, seq)
            xt = _ffn_layer(xt, fpre, fpost, w1, w2, attn=(o, b_w_out[j].astype(BF16), gpost))
    return xt.reshape(batch, seq, d)
```

```python
import functools
import math

import jax
import jax.numpy as jnp
from jax import lax
from jax.experimental import pallas as pl
from jax.experimental.pallas import tpu as pltpu

F32 = jnp.float32
BF16 = jnp.bfloat16

EPS = 1e-6
LOG2_E = math.log2(math.e)
LANES = 128
CHUNK = 128
A_GROUPS = 8
B_HEADS = 16
HEADS_PER_BLOCK = 2
ATT_LANE_BLOCKS = 4
ATT_TILE = 256
Q_HALVES = 2
DEAD_CARRY = -160.0
TOKEN_TILE = 1024
SUB_TILE = 256
VMEM_LIMIT_BYTES = 56 * 1024 * 1024


def _rms_norm(x, g):
    return x * lax.rsqrt(jnp.mean(x * x, axis=-1, keepdims=True) + EPS) * g


def _resident(shape):
    zeros = (0,) * len(shape)
    return pl.BlockSpec(shape, lambda *_: zeros, pipeline_mode=pl.Buffered(1))


def _neg_abs(x):
    bits = lax.bitcast_convert_type(x, jnp.uint32) | jnp.uint32(0x80000000)
    return lax.bitcast_convert_type(bits, F32)


def _emit_pipelined(stages, n):
    for step in range(n + len(stages) - 1):
        for depth in reversed(range(len(stages))):
            i = step - depth
            if 0 <= i < n:
                stages[depth](i)


def _params(*semantics):
    return pltpu.CompilerParams(dimension_semantics=semantics,
                                vmem_limit_bytes=VMEM_LIMIT_BYTES)


def _gmlp_kernel(x_ref, gpre_ref, gpost_ref, win_ref, vg_ref, vb_ref, ws_ref, bs_ref,
                 wout_ref, o_ref, gated_ref):
    tm = x_ref.shape[0]
    width = wout_ref.shape[0]
    gdim = width // A_GROUPS
    t_idx = lax.broadcasted_iota(jnp.int32, (CHUNK, CHUNK), 0)
    s_idx = lax.broadcasted_iota(jnp.int32, (CHUNK, CHUNK), 1)
    causal = t_idx >= s_idx
    ws = [jnp.where(causal, ws_ref[g], 0.0).astype(BF16) for g in range(A_GROUPS)]
    xs, zs, us, vns = {}, {}, {}, {}

    def rows(i):
        return slice(i * SUB_TILE, (i + 1) * SUB_TILE)

    def in_proj(i):
        xs[i] = x_ref[rows(i), :]
        h = _rms_norm(xs[i], gpre_ref[...]).astype(BF16)
        zs[i] = jnp.dot(h, win_ref[...], preferred_element_type=F32)

    def gate_norm(i):
        z = zs[i]
        z = 0.5 * z * (1.0 + lax.erf(z * (1.0 / math.sqrt(2.0))))
        us[i] = z[:, :width]
        v = z[:, width:]
        mu = jnp.mean(v, axis=-1, keepdims=True)
        vc = v - mu
        var = jnp.mean(vc * vc, axis=-1, keepdims=True)
        vns[i] = (vc * lax.rsqrt(var + EPS) * vg_ref[...] + vb_ref[...]).astype(BF16)

    def mix(i):
        for g in range(A_GROUPS):
            cols = slice(g * gdim, (g + 1) * gdim)
            for c in range(SUB_TILE // CHUNK):
                r = slice(c * CHUNK, (c + 1) * CHUNK)
                mixed = jnp.dot(ws[g], vns[i][r, cols], preferred_element_type=F32) + bs_ref[:, cols]
                gated_ref[i * SUB_TILE + c * CHUNK:i * SUB_TILE + (c + 1) * CHUNK, cols] = (
                    us[i][r, cols] * mixed).astype(BF16)

    def out_proj(i):
        m = jnp.dot(gated_ref[rows(i), :], wout_ref[...], preferred_element_type=F32)
        o_ref[rows(i), :] = xs[i] + _rms_norm(m, gpost_ref[...])

    _emit_pipelined((in_proj, gate_norm, mix, out_proj), tm // SUB_TILE)


def _gmlp_layer(x, gpre, gpost, w_in, v_g, v_b, w_s, bias_full, w_out):
    t, d = x.shape
    width = w_out.shape[0]
    tm = TOKEN_TILE
    row = pl.BlockSpec((tm, d), lambda i: (i, 0))
    return pl.pallas_call(
        _gmlp_kernel,
        out_shape=jax.ShapeDtypeStruct((t, d), F32),
        grid=(t // tm,),
        in_specs=[row, _resident((1, d)), _resident((1, d)), _resident(w_in.shape),
                  _resident((1, width)), _resident((1, width)), _resident(w_s.shape),
                  _resident(bias_full.shape), _resident(w_out.shape)],
        out_specs=row,
        scratch_shapes=[pltpu.VMEM((tm, width), BF16)],
        compiler_params=_params("parallel"),
        name="gmlp_mixer",
    )(x, gpre, gpost, w_in, v_g, v_b, w_s, bias_full, w_out)


def _ffn_kernel(*refs, with_attn_proj):
    if with_attn_proj:
        x_ref, a_ref, wo_ref, gmix_ref, gpre_ref, gpost_ref, w1_ref, w2_ref, o_ref = refs
    else:
        x_ref, gpre_ref, gpost_ref, w1_ref, w2_ref, o_ref = refs
    ms, xs, hidden = {}, {}, {}

    def rows(i):
        return slice(i * SUB_TILE, (i + 1) * SUB_TILE)

    def attn_proj(i):
        ms[i] = jnp.dot(a_ref[rows(i), :], wo_ref[...], preferred_element_type=F32)

    def up_proj(i):
        xs[i] = x_ref[rows(i), :]
        if with_attn_proj:
            xs[i] = xs[i] + _rms_norm(ms[i], gmix_ref[...])
        h = _rms_norm(xs[i], gpre_ref[...]).astype(BF16)
        hidden[i] = jnp.dot(h, w1_ref[...], preferred_element_type=F32)

    def down_proj(i):
        a = jnp.square(jnp.maximum(hidden[i], 0.0)).astype(BF16)
        f = jnp.dot(a, w2_ref[...], preferred_element_type=F32)
        o_ref[rows(i), :] = xs[i] + _rms_norm(f, gpost_ref[...])

    stages = (attn_proj, up_proj, down_proj) if with_attn_proj else (up_proj, down_proj)
    _emit_pipelined(stages, x_ref.shape[0] // SUB_TILE)


def _ffn_layer(x, gpre, gpost, w1, w2, attn=None):
    t, d = x.shape
    tm = TOKEN_TILE
    row = pl.BlockSpec((tm, d), lambda i: (i, 0))
    vec = _resident((1, d))
    if attn is None:
        args = (x, gpre, gpost, w1, w2)
        in_specs = [row, vec, vec, _resident(w1.shape), _resident(w2.shape)]
    else:
        o, w_o, gmix = attn
        args = (x, o, w_o, gmix, gpre, gpost, w1, w2)
        in_specs = [row, row, _resident(w_o.shape), vec, vec, vec,
                    _resident(w1.shape), _resident(w2.shape)]
    return pl.pallas_call(
        functools.partial(_ffn_kernel, with_attn_proj=attn is not None),
        out_shape=jax.ShapeDtypeStruct((t, d), F32),
        grid=(t // tm,),
        in_specs=in_specs,
        out_specs=row,
        compiler_params=_params("parallel"),
        name="ffn_attnproj" if attn is not None else "ffn",
    )(*args)


def _qkv_kernel(x_ref, g_ref, w_ref, q_ref, k_ref, v_ref, *, q_scale):
    d = x_ref.shape[1]
    h = _rms_norm(x_ref[...], g_ref[...]).astype(BF16)
    qkv = jnp.dot(h, w_ref[...], preferred_element_type=F32)
    q_ref[...] = (qkv[:, :d] * q_scale).astype(BF16)
    k_ref[...] = qkv[:, d:2 * d].astype(BF16)
    v_ref[...] = qkv[:, 2 * d:].astype(BF16)


def _qkv_proj(x, g, w_qkv, q_scale):
    t, d = x.shape
    tm = TOKEN_TILE
    row = pl.BlockSpec((tm, d), lambda i: (i, 0))
    out = jax.ShapeDtypeStruct((t, d), BF16)
    return pl.pallas_call(
        functools.partial(_qkv_kernel, q_scale=q_scale),
        out_shape=(out, out, out),
        grid=(t // tm,),
        in_specs=[row, _resident((1, d)), _resident(w_qkv.shape)],
        out_specs=(row, row, row),
        compiler_params=_params("parallel"),
        name="qkv_proj",
    )(x, g, w_qkv)


def _stick_tiles(qs, ks, vs, upper, masks):
    n = len(qs)
    contract_last = (((1,), (1,)), ((), ()))
    z2s, s2_bf, log_sig, first_col, later, ps, pvs = ({} for _ in range(7))

    def scores(i):
        z2s[i] = lax.dot_general(qs[i], ks[i], contract_last, preferred_element_type=F32)

    def softplus(i):
        z2 = z2s[i]
        s2 = jnp.maximum(z2, 0.0) + jnp.log2(1.0 + jnp.exp2(_neg_abs(z2)))
        log_sig[i] = z2 - s2
        if masks[i] is not None:
            s2 = jnp.where(masks[i], s2, 0.0)
        s2_bf[i] = s2.astype(BF16)
        first_col[i] = s2[:, 0:1]

    def later_sums(i):
        later[i] = jnp.dot(s2_bf[i], upper, preferred_element_type=F32)

    def weights(i):
        p = jnp.exp2(log_sig[i] - later[i])
        if masks[i] is not None:
            p = jnp.where(masks[i], p, 0.0)
        ps[i] = p.astype(BF16)

    def values(i):
        pvs[i] = jnp.dot(ps[i], vs[i], preferred_element_type=F32)

    _emit_pipelined((scores, softplus, later_sums, weights, values), n)
    tile_sums = [first_col[i] + later[i][:, 0:1] for i in range(n)]
    return [pvs[i] for i in range(n)], tile_sums


def _attn_kernel(q_ref, k_ref, v_ref, o_ref, qm_ref, acc_ref, carry_ref):
    seq = q_ref.shape[0]
    n_blocks = q_ref.shape[1] // LANES
    head_dim = LANES // HEADS_PER_BLOCK
    tq = ATT_TILE
    first_head = lax.broadcasted_iota(jnp.int32, (tq, LANES), 1) < head_dim
    j_idx = lax.broadcasted_iota(jnp.int32, (tq, tq), 0)
    s_idx = lax.broadcasted_iota(jnp.int32, (tq, tq), 1)
    upper = (j_idx > s_idx).astype(BF16)
    strict = s_idx < j_idx

    head_lane = lax.broadcasted_iota(jnp.int32, (seq, LANES), 1) < head_dim
    for blk in range(n_blocks):
        q = q_ref[:, blk * LANES:(blk + 1) * LANES]
        zero = jnp.zeros_like(q)
        qm_ref[HEADS_PER_BLOCK * blk] = jnp.where(head_lane, q, zero)
        qm_ref[HEADS_PER_BLOCK * blk + 1] = jnp.where(head_lane, zero, q)

    def run(q0, items):
        qs, ks, vs, masks = [], [], [], []
        for half, blk, k0, diagonal, _ in items:
            lanes = slice(blk * LANES, (blk + 1) * LANES)
            for h in range(HEADS_PER_BLOCK):
                qs.append(qm_ref[HEADS_PER_BLOCK * blk + h, pl.ds(q0 + half * tq, tq), :])
                ks.append(k_ref[pl.ds(k0, tq), lanes])
                vs.append(v_ref[pl.ds(k0, tq), lanes])
                masks.append(strict if diagonal else None)
        pvs, tile_sums = _stick_tiles(qs, ks, vs, upper, masks)
        acc, carry = {}, {}
        for n, (half, blk, _, _, first) in enumerate(items):
            heads = [HEADS_PER_BLOCK * blk + h for h in range(HEADS_PER_BLOCK)]
            pv = jnp.where(first_head, pvs[HEADS_PER_BLOCK * n], pvs[HEADS_PER_BLOCK * n + 1])
            if first:
                acc[half, blk] = pv
            else:
                for h in heads:
                    if (half, h) not in carry:
                        carry[half, h] = carry_ref[half, h]
                if (half, blk) not in acc:
                    acc[half, blk] = acc_ref[half, blk]
                decay = jnp.exp2(jnp.where(first_head, carry[half, heads[0]], carry[half, heads[1]]))
                acc[half, blk] = acc[half, blk] + decay * pv
            for j, h in enumerate(heads):
                tile_sum = jnp.broadcast_to(tile_sums[HEADS_PER_BLOCK * n + j], (tq, LANES))
                carry[half, h] = -tile_sum if first else carry[half, h] - tile_sum
        for key, value in acc.items():
            acc_ref[key] = value
        for key, value in carry.items():
            carry_ref[key] = value
        return carry

    blocks = range(n_blocks)

    def alive(carries, half):
        tops = [c for (hf, _), c in carries.items() if hf == half]
        return jnp.max(functools.reduce(jnp.maximum, tops)) > DEAD_CARRY

    def newest(q0, k_hi):
        return ([(1, b, k_hi, True, True) for b in blocks]
                + [(0, b, q0, True, True) for b in blocks]
                + [(1, b, q0, False, False) for b in blocks])

    def store(q0):
        for half in range(Q_HALVES):
            for b in blocks:
                o_ref[pl.ds(q0 + half * tq, tq), b * LANES:(b + 1) * LANES] = (
                    acc_ref[half, b].astype(o_ref.dtype))

    run(0, newest(0, tq))
    store(0)

    def q_tile(m, _):
        q0 = pl.multiple_of(m * (Q_HALVES * tq), Q_HALVES * tq)
        k_lo = pl.multiple_of(q0 - tq, tq)
        carries = run(q0, newest(q0, pl.multiple_of(q0 + tq, tq))
                      + [(0, b, k_lo, False, False) for b in blocks])

        for half in range(Q_HALVES):
            n_older = Q_HALVES * m - 1 + half

            def k_tile(state, half=half, n_older=n_older):
                i, _ = state
                k0 = pl.multiple_of((n_older - 1 - i) * tq, tq)
                return i + 1, alive(run(q0, [(half, b, k0, False, False) for b in blocks]), half)

            lax.while_loop(lambda state, n_older=n_older:
                           jnp.logical_and(state[0] < n_older, state[1]),
                           k_tile, (jnp.int32(0), alive(carries, half)))
        store(q0)
        return 0

    lax.fori_loop(1, seq // (Q_HALVES * tq), q_tile, 0)


def _attn_core(q, k, v, batch, seq):
    t, d = q.shape
    assert LANES == HEADS_PER_BLOCK * (d // B_HEADS)
    width = ATT_LANE_BLOCKS * LANES
    blk = pl.BlockSpec((seq, width), lambda b, hp: (b, hp))
    n_heads = ATT_LANE_BLOCKS * HEADS_PER_BLOCK
    return pl.pallas_call(
        _attn_kernel,
        out_shape=jax.ShapeDtypeStruct((t, d), BF16),
        grid=(batch, d // width),
        in_specs=[blk, blk, blk],
        out_specs=blk,
        scratch_shapes=[pltpu.VMEM((n_heads, seq, LANES), BF16),
                        pltpu.VMEM((Q_HALVES, ATT_LANE_BLOCKS, ATT_TILE, LANES), F32),
                        pltpu.VMEM((Q_HALVES, n_heads, ATT_TILE, LANES), F32)],
        compiler_params=_params("parallel", "parallel"),
        name="stick_attn",
    )(q, k, v)


def kernel(x, norm_mix_pre, norm_mix_post, norm_ffn_pre, norm_ffn_post, a_w_in, a_v_g, a_v_b, a_w_s, a_b_s, a_w_out, b_w_qkv, b_w_out, mlp_w1, mlp_w2):
    batch, seq, d = x.shape
    depth = norm_mix_pre.shape[0]
    width = a_w_out.shape[1]
    assert seq % (Q_HALVES * ATT_TILE) == 0 and (batch * seq) % TOKEN_TILE == 0 and TOKEN_TILE % CHUNK == 0
    assert a_w_s.shape[1:] == (A_GROUPS, CHUNK, CHUNK)
    q_scale = (d // B_HEADS) ** -0.5 * LOG2_E

    row = lambda p, i: p[i][None, :]
    xt = x.reshape(batch * seq, d)
    for i in range(depth):
        j = i // 2
        gpre, gpost = row(norm_mix_pre, i), row(norm_mix_post, i)
        fpre, fpost = row(norm_ffn_pre, i), row(norm_ffn_post, i)
        w1, w2 = mlp_w1[i].astype(BF16), mlp_w2[i].astype(BF16)
        if i % 2 == 0:
            bias_full = jnp.repeat(a_b_s[j].T, width // A_GROUPS, axis=1)
            xt = _gmlp_layer(xt, gpre, gpost, a_w_in[j].astype(BF16), row(a_v_g, j), row(a_v_b, j),
                             a_w_s[j], bias_full, a_w_out[j].astype(BF16))
            xt = _ffn_layer(xt, fpre, fpost, w1, w2)
        else:
            q, k, v = _qkv_proj(xt, gpre, b_w_qkv[j].astype(BF16), q_scale)
            o = _attn_core(q, k, v, batch, seq)
            xt = _ffn_layer(xt, fpre, fpost, w1, w2, attn=(o, b_w_out[j].astype(BF16), gpost))
    return xt.reshape(batch, seq, d)
```

```python
import functools
import math

import jax
import jax.numpy as jnp
from jax import lax
from jax.experimental import pallas as pl
from jax.experimental.pallas import tpu as pltpu

F32 = jnp.float32
BF16 = jnp.bfloat16

EPS = 1e-6
LOG2_E = math.log2(math.e)
LANES = 128
CHUNK = 128
A_GROUPS = 8
B_HEADS = 16
HEADS_PER_BLOCK = 2
ATT_LANE_BLOCKS = 4
ATT_TILE = 256
Q_HALVES = 2
DEAD_CARRY = -160.0
TOKEN_TILE = 1024
SUB_TILE = 256
VMEM_LIMIT_BYTES = 56 * 1024 * 1024


def _rms_norm(x, g):
    return x * lax.rsqrt(jnp.mean(x * x, axis=-1, keepdims=True) + EPS) * g


def _resident(shape):
    zeros = (0,) * len(shape)
    return pl.BlockSpec(shape, lambda *_: zeros, pipeline_mode=pl.Buffered(1))


def _neg_abs(x):
    bits = lax.bitcast_convert_type(x, jnp.uint32) | jnp.uint32(0x80000000)
    return lax.bitcast_convert_type(bits, F32)


def _emit_pipelined(stages, n):
    for step in range(n + len(stages) - 1):
        for depth in reversed(range(len(stages))):
            i = step - depth
            if 0 <= i < n:
                stages[depth](i)


def _params(*semantics):
    return pltpu.CompilerParams(dimension_semantics=semantics,
                                vmem_limit_bytes=VMEM_LIMIT_BYTES)


def _gmlp_kernel(x_ref, gpre_ref, gpost_ref, win_ref, vg_ref, vb_ref, ws_ref, bs_ref,
                 wout_ref, o_ref, gated_ref):
    tm = x_ref.shape[0]
    width = wout_ref.shape[0]
    gdim = width // A_GROUPS
    t_idx = lax.broadcasted_iota(jnp.int32, (CHUNK, CHUNK), 0)
    s_idx = lax.broadcasted_iota(jnp.int32, (CHUNK, CHUNK), 1)
    causal = t_idx >= s_idx
    ws = [jnp.where(causal, ws_ref[g], 0.0).astype(BF16) for g in range(A_GROUPS)]
    xs, zs, us, vns = {}, {}, {}, {}

    def rows(i):
        return slice(i * SUB_TILE, (i + 1) * SUB_TILE)

    def in_proj(i):
        xs[i] = x_ref[rows(i), :]
        h = _rms_norm(xs[i], gpre_ref[...]).astype(BF16)
        zs[i] = jnp.dot(h, win_ref[...], preferred_element_type=F32)

    def gate_norm(i):
        z = zs[i]
        z = 0.5 * z * (1.0 + lax.erf(z * (1.0 / math.sqrt(2.0))))
        us[i] = z[:, :width]
        v = z[:, width:]
        mu = jnp.mean(v, axis=-1, keepdims=True)
        vc = v - mu
        var = jnp.mean(vc * vc, axis=-1, keepdims=True)
        vns[i] = (vc * lax.rsqrt(var + EPS) * vg_ref[...] + vb_ref[...]).astype(BF16)

    def mix(i):
        for g in range(A_GROUPS):
            cols = slice(g * gdim, (g + 1) * gdim)
            for c in range(SUB_TILE // CHUNK):
                r = slice(c * CHUNK, (c + 1) * CHUNK)
                mixed = jnp.dot(ws[g], vns[i][r, cols], preferred_element_type=F32) + bs_ref[:, cols]
                gated_ref[i * SUB_TILE + c * CHUNK:i * SUB_TILE + (c + 1) * CHUNK, cols] = (
                    us[i][r, cols] * mixed).astype(BF16)

    def out_proj(i):
        m = jnp.dot(gated_ref[rows(i), :], wout_ref[...], preferred_element_type=F32)
        o_ref[rows(i), :] = xs[i] + _rms_norm(m, gpost_ref[...])

    _emit_pipelined((in_proj, gate_norm, mix, out_proj), tm // SUB_TILE)


def _gmlp_layer(x, gpre, gpost, w_in, v_g, v_b, w_s, bias_full, w_out):
    t, d = x.shape
    width = w_out.shape[0]
    tm = TOKEN_TILE
    row = pl.BlockSpec((tm, d), lambda i: (i, 0))
    return pl.pallas_call(
        _gmlp_kernel,
        out_shape=jax.ShapeDtypeStruct((t, d), F32),
        grid=(t // tm,),
        in_specs=[row, _resident((1, d)), _resident((1, d)), _resident(w_in.shape),
                  _resident((1, width)), _resident((1, width)), _resident(w_s.shape),
                  _resident(bias_full.shape), _resident(w_out.shape)],
        out_specs=row,
        scratch_shapes=[pltpu.VMEM((tm, width), BF16)],
        compiler_params=_params("parallel"),
        name="gmlp_mixer",
    )(x, gpre, gpost, w_in, v_g, v_b, w_s, bias_full, w_out)


def _ffn_kernel(*refs, with_attn_proj):
    if with_attn_proj:
        x_ref, a_ref, wo_ref, gmix_ref, gpre_ref, gpost_ref, w1_ref, w2_ref, o_ref = refs
    else:
        x_ref, gpre_ref, gpost_ref, w1_ref, w2_ref, o_ref = refs
    ms, xs, hidden = {}, {}, {}

    def rows(i):
        return slice(i * SUB_TILE, (i + 1) * SUB_TILE)

    def attn_proj(i):
        ms[i] = jnp.dot(a_ref[rows(i), :], wo_ref[...], preferred_element_type=F32)

    def up_proj(i):
        xs[i] = x_ref[rows(i), :]
        if with_attn_proj:
            xs[i] = xs[i] + _rms_norm(ms[i], gmix_ref[...])
        h = _rms_norm(xs[i], gpre_ref[...]).astype(BF16)
        hidden[i] = jnp.dot(h, w1_ref[...], preferred_element_type=F32)

    def down_proj(i):
        a = jnp.square(jnp.maximum(hidden[i], 0.0)).astype(BF16)
        f = jnp.dot(a, w2_ref[...], preferred_element_type=F32)
        o_ref[rows(i), :] = xs[i] + _rms_norm(f, gpost_ref[...])

    stages = (attn_proj, up_proj, down_proj) if with_attn_proj else (up_proj, down_proj)
    _emit_pipelined(stages, x_ref.shape[0] // SUB_TILE)


def _ffn_layer(x, gpre, gpost, w1, w2, attn=None):
    t, d = x.shape
    tm = TOKEN_TILE
    row = pl.BlockSpec((tm, d), lambda i: (i, 0))
    vec = _resident((1, d))
    if attn is None:
        args = (x, gpre, gpost, w1, w2)
        in_specs = [row, vec, vec, _resident(w1.shape), _resident(w2.shape)]
    else:
        o, w_o, gmix = attn
        args = (x, o, w_o, gmix, gpre, gpost, w1, w2)
        in_specs = [row, row, _resident(w_o.shape), vec, vec, vec,
                    _resident(w1.shape), _resident(w2.shape)]
    return pl.pallas_call(
        functools.partial(_ffn_kernel, with_attn_proj=attn is not None),
        out_shape=jax.ShapeDtypeStruct((t, d), F32),
        grid=(t // tm,),
        in_specs=in_specs,
        out_specs=row,
        compiler_params=_params("parallel"),
        name="ffn_attnproj" if attn is not None else "ffn",
    )(*args)


def _qkv_kernel(x_ref, g_ref, w_ref, qa_ref, qb_ref, k_ref, v_ref, *, q_scale):
    d = x_ref.shape[1]
    h = _rms_norm(x_ref[...], g_ref[...]).astype(BF16)
    qkv = jnp.dot(h, w_ref[...], preferred_element_type=F32)
    q = (qkv[:, :d] * q_scale).astype(BF16)
    head = lax.broadcasted_iota(jnp.int32, q.shape, 1) // (LANES // HEADS_PER_BLOCK)
    first_head = head % HEADS_PER_BLOCK == 0
    zero = jnp.zeros_like(q)
    qa_ref[...] = jnp.where(first_head, q, zero)
    qb_ref[...] = jnp.where(first_head, zero, q)
    k_ref[...] = qkv[:, d:2 * d].astype(BF16)
    v_ref[...] = qkv[:, 2 * d:].astype(BF16)


def _qkv_proj(x, g, w_qkv, q_scale):
    t, d = x.shape
    tm = TOKEN_TILE
    row = pl.BlockSpec((tm, d), lambda i: (i, 0))
    out = jax.ShapeDtypeStruct((t, d), BF16)
    return pl.pallas_call(
        functools.partial(_qkv_kernel, q_scale=q_scale),
        out_shape=(out, out, out, out),
        grid=(t // tm,),
        in_specs=[row, _resident((1, d)), _resident(w_qkv.shape)],
        out_specs=(row, row, row, row),
        compiler_params=_params("parallel"),
        name="qkv_proj",
    )(x, g, w_qkv)


def _stick_tiles(qs, ks, vs, upper, masks):
    n = len(qs)
    contract_last = (((1,), (1,)), ((), ()))
    z2s, s2_bf, log_sig, first_col, later, ps, pvs = ({} for _ in range(7))

    def scores(i):
        z2s[i] = lax.dot_general(qs[i], ks[i], contract_last, preferred_element_type=F32)

    def softplus(i):
        z2 = z2s[i]
        s2 = jnp.maximum(z2, 0.0) + jnp.log2(1.0 + jnp.exp2(_neg_abs(z2)))
        log_sig[i] = z2 - s2
        if masks[i] is not None:
            s2 = jnp.where(masks[i], s2, 0.0)
        s2_bf[i] = s2.astype(BF16)
        first_col[i] = s2[:, 0:1]

    def later_sums(i):
        later[i] = jnp.dot(s2_bf[i], upper, preferred_element_type=F32)

    def weights(i):
        p = jnp.exp2(log_sig[i] - later[i])
        if masks[i] is not None:
            p = jnp.where(masks[i], p, 0.0)
        ps[i] = p.astype(BF16)

    def values(i):
        pvs[i] = jnp.dot(ps[i], vs[i], preferred_element_type=F32)

    _emit_pipelined((scores, softplus, later_sums, weights, values), n)
    tile_sums = [first_col[i] + later[i][:, 0:1] for i in range(n)]
    return [pvs[i] for i in range(n)], tile_sums


def _attn_kernel(qa_ref, qb_ref, k_ref, v_ref, o_ref, acc_ref, carry_ref):
    q_refs = (qa_ref, qb_ref)
    assert len(q_refs) == HEADS_PER_BLOCK
    seq = k_ref.shape[0]
    n_blocks = k_ref.shape[1] // LANES
    head_dim = LANES // HEADS_PER_BLOCK
    tq = ATT_TILE
    first_head = lax.broadcasted_iota(jnp.int32, (tq, LANES), 1) < head_dim
    j_idx = lax.broadcasted_iota(jnp.int32, (tq, tq), 0)
    s_idx = lax.broadcasted_iota(jnp.int32, (tq, tq), 1)
    upper = (j_idx > s_idx).astype(BF16)
    strict = s_idx < j_idx

    def run(q0, items):
        qs, ks, vs, masks = [], [], [], []
        for half, blk, k0, diagonal, _ in items:
            lanes = slice(blk * LANES, (blk + 1) * LANES)
            for h in range(HEADS_PER_BLOCK):
                qs.append(q_refs[h][pl.ds(q0 + half * tq, tq), lanes])
                ks.append(k_ref[pl.ds(k0, tq), lanes])
                vs.append(v_ref[pl.ds(k0, tq), lanes])
                masks.append(strict if diagonal else None)
        pvs, tile_sums = _stick_tiles(qs, ks, vs, upper, masks)
        acc, carry = {}, {}
        for n, (half, blk, _, _, first) in enumerate(items):
            heads = [HEADS_PER_BLOCK * blk + h for h in range(HEADS_PER_BLOCK)]
            pv = jnp.where(first_head, pvs[HEADS_PER_BLOCK * n], pvs[HEADS_PER_BLOCK * n + 1])
            if first:
                acc[half, blk] = pv
            else:
                for h in heads:
                    if (half, h) not in carry:
                        carry[half, h] = carry_ref[half, h]
                if (half, blk) not in acc:
                    acc[half, blk] = acc_ref[half, blk]
                decay = jnp.exp2(jnp.where(first_head, carry[half, heads[0]], carry[half, heads[1]]))
                acc[half, blk] = acc[half, blk] + decay * pv
            for j, h in enumerate(heads):
                tile_sum = jnp.broadcast_to(tile_sums[HEADS_PER_BLOCK * n + j], (tq, LANES))
                carry[half, h] = -tile_sum if first else carry[half, h] - tile_sum
        for key, value in acc.items():
            acc_ref[key] = value
        for key, value in carry.items():
            carry_ref[key] = value
        return carry

    blocks = range(n_blocks)

    def alive(carries, half):
        tops = [c for (hf, _), c in carries.items() if hf == half]
        return jnp.max(functools.reduce(jnp.maximum, tops)) > DEAD_CARRY

    def newest(q0, k_hi):
        return ([(1, b, k_hi, True, True) for b in blocks]
                + [(0, b, q0, True, True) for b in blocks]
                + [(1, b, q0, False, False) for b in blocks])

    def store(q0):
        for half in range(Q_HALVES):
            for b in blocks:
                o_ref[pl.ds(q0 + half * tq, tq), b * LANES:(b + 1) * LANES] = (
                    acc_ref[half, b].astype(o_ref.dtype))

    run(0, newest(0, tq))
    store(0)

    def q_tile(m, _):
        q0 = pl.multiple_of(m * (Q_HALVES * tq), Q_HALVES * tq)
        k_lo = pl.multiple_of(q0 - tq, tq)
        carries = run(q0, newest(q0, pl.multiple_of(q0 + tq, tq))
                      + [(0, b, k_lo, False, False) for b in blocks])

        for half in range(Q_HALVES):
            n_older = Q_HALVES * m - 1 + half

            def k_tile(state, half=half, n_older=n_older):
                i, _ = state
                k0 = pl.multiple_of((n_older - 1 - i) * tq, tq)
                return i + 1, alive(run(q0, [(half, b, k0, False, False) for b in blocks]), half)

            lax.while_loop(lambda state, n_older=n_older:
                           jnp.logical_and(state[0] < n_older, state[1]),
                           k_tile, (jnp.int32(0), alive(carries, half)))
        store(q0)
        return 0

    lax.fori_loop(1, seq // (Q_HALVES * tq), q_tile, 0)


def _attn_core(qa, qb, k, v, batch, seq):
    t, d = k.shape
    assert LANES == HEADS_PER_BLOCK * (d // B_HEADS)
    width = ATT_LANE_BLOCKS * LANES
    blk = pl.BlockSpec((seq, width), lambda b, hp: (b, hp))
    n_heads = ATT_LANE_BLOCKS * HEADS_PER_BLOCK
    return pl.pallas_call(
        _attn_kernel,
        out_shape=jax.ShapeDtypeStruct((t, d), BF16),
        grid=(batch, d // width),
        in_specs=[blk, blk, blk, blk],
        out_specs=blk,
        scratch_shapes=[pltpu.VMEM((Q_HALVES, ATT_LANE_BLOCKS, ATT_TILE, LANES), F32),
                        pltpu.VMEM((Q_HALVES, n_heads, ATT_TILE, LANES), F32)],
        compiler_params=_params("parallel", "parallel"),
        name="stick_attn",
    )(qa, qb, k, v)


def kernel(x, norm_mix_pre, norm_mix_post, norm_ffn_pre, norm_ffn_post, a_w_in, a_v_g, a_v_b, a_w_s, a_b_s, a_w_out, b_w_qkv, b_w_out, mlp_w1, mlp_w2):
    batch, seq, d = x.shape
    depth = norm_mix_pre.shape[0]
    width = a_w_out.shape[1]
    assert seq % (Q_HALVES * ATT_TILE) == 0 and (batch * seq) % TOKEN_TILE == 0 and TOKEN_TILE % CHUNK == 0
    assert a_w_s.shape[1:] == (A_GROUPS, CHUNK, CHUNK)
    q_scale = (d // B_HEADS) ** -0.5 * LOG2_E

    row = lambda p, i: p[i][None, :]
    xt = x.reshape(batch * seq, d)
    for i in range(depth):
        j = i // 2
        gpre, gpost = row(norm_mix_pre, i), row(norm_mix_post, i)
        fpre, fpost = row(norm_ffn_pre, i), row(norm_ffn_post, i)
        w1, w2 = mlp_w1[i].astype(BF16), mlp_w2[i].astype(BF16)
        if i % 2 == 0:
            bias_full = jnp.repeat(a_b_s[j].T, width // A_GROUPS, axis=1)
            xt = _gmlp_layer(xt, gpre, gpost, a_w_in[j].astype(BF16), row(a_v_g, j), row(a_v_b, j),
                             a_w_s[j], bias_full, a_w_out[j].astype(BF16))
            xt = _ffn_layer(xt, fpre, fpost, w1, w2)
        else:
            qa, qb, k, v = _qkv_proj(xt, gpre, b_w_qkv[j].astype(BF16), q_scale)
            o = _attn_core(qa, qb, k, v, batch, seq)
            xt = _ffn_layer(xt, fpre, fpost, w1, w2, attn=(o, b_w_out[j].astype(BF16), gpost))
    return xt.reshape(batch, seq, d)
```
